```python
import jax
import jax.numpy as jnp
from jax import lax
import numpy as np

D_MODEL = 2048
BATCH = 8
SEQ = 2048
DEPTH = 2

N_META = 16
N_HEADS = 16
HEAD_DIM = 128
D_ATTN = N_HEADS * HEAD_DIM
Q_BLOCK = 128
CONV_WIDTH = 31
D_FF = 5632
N_EXPERTS = 8
TOP_K = 2
D_FF_EXPERT = 5632
RMS_EPS = 1e-6
LN_EPS = 1e-5

N_A_LAYERS = DEPTH // 2
N_B_LAYERS = DEPTH - N_A_LAYERS
N_DENSE_LAYERS = (DEPTH + 1) // 2
N_MOE_LAYERS = DEPTH // 2

kernel_name = "yoco_conformer_stickbreaking_moe_trunk"


def rms_norm(x, g):
    xf = x.astype(jnp.float32)
    y = xf * lax.rsqrt(jnp.mean(xf * xf, axis=-1, keepdims=True) + RMS_EPS)
    return (y * g.astype(jnp.float32)).astype(x.dtype)


def layer_norm(x, g, b):
    xf = x.astype(jnp.float32)
    mu = jnp.mean(xf, axis=-1, keepdims=True)
    var = jnp.mean(jnp.square(xf - mu), axis=-1, keepdims=True)
    y = (xf - mu) * lax.rsqrt(var + LN_EPS)
    return (y * g.astype(jnp.float32) + b.astype(jnp.float32)).astype(x.dtype)


def swiglu(h, w_gate, w_up, w_down):
    a = jnp.einsum('bld,df->blf', h, w_gate)
    u = jnp.einsum('bld,df->blf', h, w_up)
    return jnp.einsum('blf,fd->bld', jax.nn.silu(a) * u, w_down)


def moe_swiglu(h, w_router, w_gate, w_up, w_down):
    logits = jnp.einsum('bld,de->ble', h, w_router).astype(jnp.float32)
    top_vals, top_idx = lax.top_k(logits, TOP_K)
    gates = jax.nn.softmax(top_vals, axis=-1)
    combine = jnp.sum(jax.nn.one_hot(top_idx, N_EXPERTS, dtype=jnp.float32)
                      * gates[..., None], axis=-2)
    y = jnp.zeros_like(h)
    for e in range(N_EXPERTS):
        y = y + combine[..., e:e + 1].astype(h.dtype) * swiglu(h, w_gate[e], w_up[e], w_down[e])
    return y


def conformer_conv(h, pw1_w, pw1_b, dw_w, dw_b, ln_g, ln_b, pw2_w, pw2_b):
    u = jnp.einsum('bld,de->ble', h, pw1_w) + pw1_b
    a, g = jnp.split(u, 2, axis=-1)
    u = a * jax.nn.sigmoid(g)
    u = lax.conv_general_dilated(
        u, dw_w[:, None, :], window_strides=(1,),
        padding=[(CONV_WIDTH - 1, 0)],
        dimension_numbers=('NWC', 'WIO', 'NWC'),
        feature_group_count=D_MODEL) + dw_b
    u = jax.nn.silu(layer_norm(u, ln_g, ln_b))
    return jnp.einsum('bld,de->ble', u, pw2_w) + pw2_b


def stick_breaking_attention(q, k, v):
    L = q.shape[2]
    Lp = -(-L // Q_BLOCK) * Q_BLOCK
    pad = [(0, 0), (0, 0), (0, Lp - L), (0, 0)]
    q = jnp.pad(q, pad)
    k = jnp.pad(k, pad)
    v = jnp.pad(v, pad)
    scale = HEAD_DIM ** -0.5
    outs = []
    for blk in range(Lp // Q_BLOCK):
        t0 = blk * Q_BLOCK
        t1 = t0 + Q_BLOCK
        z = jnp.einsum('bhtd,bhsd->bhts', q[:, :, t0:t1], k[:, :, :t1]).astype(jnp.float32) * scale
        t_pos = t0 + jnp.arange(Q_BLOCK)[:, None]
        s_pos = jnp.arange(t1)[None, :]
        causal = s_pos < t_pos
        log_stay = jnp.where(causal, jax.nn.log_sigmoid(-z), 0.0)
        suffix = lax.cumsum(log_stay, axis=3, reverse=True)
        between = jnp.concatenate([suffix[..., 1:], jnp.zeros_like(suffix[..., :1])], axis=-1)
        weights = jnp.where(causal, jnp.exp(jax.nn.log_sigmoid(z) + between), 0.0)
        outs.append(jnp.einsum('bhts,bhsd->bhtd', weights.astype(v.dtype), v[:, :, :t1]))
    return jnp.concatenate(outs, axis=2)[:, :, :L]


def setup_inputs(seed: int = 0) -> dict:
    key = jax.random.key(seed)
    ks = iter(jax.random.split(key, 32))
    f32 = jnp.float32

    def nrm(shape, scale):
        return jax.random.normal(next(ks), shape, f32) * scale

    D = D_MODEL
    return {
        "x": nrm((BATCH, SEQ, D), 1.0),
        "meta_tokens": nrm((N_META, D), 1.0),
        "mix_norm_g": 1.0 + nrm((DEPTH, D), 0.02),
        "ffn_norm_g": 1.0 + nrm((DEPTH, D), 0.02),
        "conv_pw1_w": nrm((N_A_LAYERS, D, 2 * D), D ** -0.5),
        "conv_pw1_b": nrm((N_A_LAYERS, 2 * D), 0.02),
        "conv_dw_w": nrm((N_A_LAYERS, CONV_WIDTH, D), CONV_WIDTH ** -0.5),
        "conv_dw_b": nrm((N_A_LAYERS, D), 0.02),
        "conv_ln_g": 1.0 + nrm((N_A_LAYERS, D), 0.02),
        "conv_ln_b": nrm((N_A_LAYERS, D), 0.02),
        "conv_pw2_w": nrm((N_A_LAYERS, D, D), D ** -0.5),
        "conv_pw2_b": nrm((N_A_LAYERS, D), 0.02),
        "kv_norm_g": 1.0 + nrm((D,), 0.02),
        "w_kv": nrm((D, 2 * D_ATTN), D ** -0.5),
        "w_q": nrm((N_B_LAYERS, D, D_ATTN), D ** -0.5),
        "w_o": nrm((N_B_LAYERS, D_ATTN, D), D_ATTN ** -0.5),
        "ffn_w_gate": nrm((N_DENSE_LAYERS, D, D_FF), D ** -0.5),
        "ffn_w_up": nrm((N_DENSE_LAYERS, D, D_FF), D ** -0.5),
        "ffn_w_down": nrm((N_DENSE_LAYERS, D_FF, D), D_FF ** -0.5),
        "moe_router": nrm((N_MOE_LAYERS, D, N_EXPERTS), D ** -0.5),
        "moe_w_gate": nrm((N_MOE_LAYERS, N_EXPERTS, D, D_FF_EXPERT), D ** -0.5),
        "moe_w_up": nrm((N_MOE_LAYERS, N_EXPERTS, D, D_FF_EXPERT), D ** -0.5),
        "moe_w_down": nrm((N_MOE_LAYERS, N_EXPERTS, D_FF_EXPERT, D), D_FF_EXPERT ** -0.5),
        "final_norm_g": 1.0 + nrm((D,), 0.02),
    }


def reference(x, meta_tokens, mix_norm_g, ffn_norm_g, conv_pw1_w, conv_pw1_b, conv_dw_w,
              conv_dw_b, conv_ln_g, conv_ln_b, conv_pw2_w, conv_pw2_b, kv_norm_g, w_kv,
              w_q, w_o, ffn_w_gate, ffn_w_up, ffn_w_down, moe_router, moe_w_gate,
              moe_w_up, moe_w_down, final_norm_g):
    B = x.shape[0]
    meta = jnp.broadcast_to(meta_tokens.astype(x.dtype)[None], (B, N_META, D_MODEL))
    h = jnp.concatenate([meta, x], axis=1)
    L = h.shape[1]

    k_shared = None
    v_shared = None
    for layer in range(DEPTH):
        hn = rms_norm(h, mix_norm_g[layer])
        if layer < N_A_LAYERS:
            a = layer
            h = h + conformer_conv(hn, conv_pw1_w[a], conv_pw1_b[a], conv_dw_w[a], conv_dw_b[a],
                                   conv_ln_g[a], conv_ln_b[a], conv_pw2_w[a], conv_pw2_b[a])
        else:
            b = layer - N_A_LAYERS
            q = jnp.einsum('bld,de->ble', hn, w_q[b]).reshape(B, L, N_HEADS, HEAD_DIM)
            o = stick_breaking_attention(q.transpose(0, 2, 1, 3), k_shared, v_shared)
            o = o.transpose(0, 2, 1, 3).reshape(B, L, D_ATTN)
            h = h + jnp.einsum('ble,ed->bld', o, w_o[b])
        hn = rms_norm(h, ffn_norm_g[layer])
        if layer % 2 == 0:
            i = layer // 2
            h = h + swiglu(hn, ffn_w_gate[i], ffn_w_up[i], ffn_w_down[i])
        else:
            i = layer // 2
            h = h + moe_swiglu(hn, moe_router[i], moe_w_gate[i], moe_w_up[i], moe_w_down[i])
        if layer == N_A_LAYERS - 1:
            kv = jnp.einsum('bld,de->ble', rms_norm(h, kv_norm_g), w_kv)
            k_s, v_s = jnp.split(kv, 2, axis=-1)
            k_shared = k_s.reshape(B, L, N_HEADS, HEAD_DIM).transpose(0, 2, 1, 3)
            v_shared = v_s.reshape(B, L, N_HEADS, HEAD_DIM).transpose(0, 2, 1, 3)

    return rms_norm(h, final_norm_g)[:, N_META:]
```

```python
import functools

import jax
import jax.numpy as jnp
from jax import lax
from jax.experimental import pallas as pl
from jax.experimental.pallas import tpu as pltpu

D_MODEL = 2048
N_META = 16
META_PAD = 32
N_HEADS = 16
HEAD_DIM = 128
CONV_WIDTH = 31
CONV_HALO = 32
D_FF = 5632
N_EXPERTS = 8
TOP_K = 2
RMS_EPS = 1e-6
LN_EPS = 1e-5

VMEM_LIMIT_BYTES = 56 * 1024 * 1024
F32 = jnp.float32
BF16 = jnp.bfloat16


def _dot(a, b):
    return jnp.dot(a, b, preferred_element_type=F32)


def _params(*sem):
    return pltpu.CompilerParams(dimension_semantics=sem, vmem_limit_bytes=VMEM_LIMIT_BYTES)


def _rms(h, g):
    var = jnp.mean(h * h, axis=-1, keepdims=True)
    return h * lax.rsqrt(var + RMS_EPS) * g


def _sigmoid(x):
    return 1.0 / (1.0 + jnp.exp(-x))


def _rms_glu_kernel(h_ref, g_ref, wa_ref, wg_ref, ba_ref, bg_ref, o_ref, hn_ref):
    @pl.when(pl.program_id(1) == 0)
    def _():
        hn_ref[...] = _rms(h_ref[...], g_ref[...]).astype(BF16)

    hn = hn_ref[...]
    a = _dot(hn, wa_ref[...]) + ba_ref[...]
    g = _dot(hn, wg_ref[...]) + bg_ref[...]
    o_ref[...] = (a * _sigmoid(g)).astype(o_ref.dtype)


def rms_glu(h, g, w, b, *, tm, tn):
    m, d = h.shape
    nb = d // tn
    return pl.pallas_call(
        _rms_glu_kernel,
        grid=(m // tm, nb),
        in_specs=[
            pl.BlockSpec((tm, d), lambda i, n: (i, 0)),
            pl.BlockSpec((1, d), lambda i, n: (0, 0)),
            pl.BlockSpec((d, tn), lambda i, n: (0, n)),
            pl.BlockSpec((d, tn), lambda i, n: (0, n + nb)),
            pl.BlockSpec((1, tn), lambda i, n: (0, n)),
            pl.BlockSpec((1, tn), lambda i, n: (0, n + nb)),
        ],
        out_specs=pl.BlockSpec((tm, tn), lambda i, n: (i, n)),
        out_shape=jax.ShapeDtypeStruct((m, d), BF16),
        scratch_shapes=[pltpu.VMEM((tm, d), BF16)],
        compiler_params=_params("parallel", "arbitrary"),
        name="rms_glu",
    )(h, g, w, w, b, b)


def _rms_proj_kernel(h_ref, g_ref, w_ref, o_ref, hn_ref, *, out_scale):
    @pl.when(pl.program_id(1) == 0)
    def _():
        hn_ref[...] = _rms(h_ref[...], g_ref[...]).astype(BF16)

    o_ref[...] = (_dot(hn_ref[...], w_ref[...]) * out_scale).astype(o_ref.dtype)


def rms_proj(h, g, w, *, tm, tn, out_scale=1.0):
    m, d = h.shape
    n = w.shape[1]
    return pl.pallas_call(
        functools.partial(_rms_proj_kernel, out_scale=out_scale),
        grid=(m // tm, n // tn),
        in_specs=[
            pl.BlockSpec((tm, d), lambda i, j: (i, 0)),
            pl.BlockSpec((1, d), lambda i, j: (0, 0)),
            pl.BlockSpec((d, tn), lambda i, j: (0, j)),
        ],
        out_specs=pl.BlockSpec((tm, tn), lambda i, j: (i, j)),
        out_shape=jax.ShapeDtypeStruct((m, n), BF16),
        scratch_shapes=[pltpu.VMEM((tm, d), BF16)],
        compiler_params=_params("parallel", "arbitrary"),
        name="rms_proj",
    )(h, g, w)


CONV_LANES = 256
CONV_ROWS = 32


def _conv_kernel(u_ref, uh_ref, halo0_ref, h_ref, dw_ref, dwb_ref, lng_ref, lnb_ref,
                 w2_ref, b2_ref, o_ref, xp_ref, sh_ref, cv_ref, *, tl):
    i = pl.program_id(1)
    d = u_ref.shape[-1]

    @pl.when(i == 0)
    def _():
        xp_ref[0:CONV_HALO, :] = halo0_ref[...].astype(F32)

    @pl.when(i > 0)
    def _():
        xp_ref[0:CONV_HALO, :] = uh_ref[...].astype(F32)

    xp_ref[CONV_HALO:CONV_HALO + tl, :] = u_ref[...].astype(F32)

    n_sh = tl + CONV_HALO - 8

    def lane_block(cb, carry):
        c0 = pl.multiple_of(cb * CONV_LANES, CONV_LANES)
        for r in range(1, 8):
            sh_ref[r, 0:n_sh, :] = xp_ref[pl.ds(r, n_sh), pl.ds(c0, CONV_LANES)]

        def row_chunk(rc, carry2):
            r0 = pl.multiple_of(rc * CONV_ROWS, CONV_ROWS)
            acc = jnp.zeros((CONV_ROWS, CONV_LANES), F32)
            for w in range(CONV_WIDTH):
                a, r = divmod(CONV_HALO - (CONV_WIDTH - 1) + w, 8)
                if r == 0:
                    xs = xp_ref[pl.ds(r0 + 8 * a, CONV_ROWS), pl.ds(c0, CONV_LANES)]
                else:
                    xs = sh_ref[r, pl.ds(r0 + 8 * a, CONV_ROWS), :]
                acc = acc + xs * dw_ref[pl.ds(w, 1), pl.ds(c0, CONV_LANES)]
            cv_ref[pl.ds(r0, CONV_ROWS), pl.ds(c0, CONV_LANES)] = acc + dwb_ref[:, pl.ds(c0, CONV_LANES)]
            return carry2

        lax.fori_loop(0, tl // CONV_ROWS, row_chunk, 0)
        return carry

    lax.fori_loop(0, d // CONV_LANES, lane_block, 0)

    c = cv_ref[...]
    mu = jnp.mean(c, axis=-1, keepdims=True)
    cc = c - mu
    var = jnp.mean(cc * cc, axis=-1, keepdims=True)
    y = cc * lax.rsqrt(var + LN_EPS) * lng_ref[...] + lnb_ref[...]
    s = (y * _sigmoid(y)).astype(BF16)
    o_ref[...] = h_ref[...] + _dot(s, w2_ref[...]) + b2_ref[...]


def conv_block(u, halo0, h, dw_w, dw_b, ln_g, ln_b, w2, b2, *, tl):
    b, l, d = u.shape
    hb = tl // CONV_HALO
    return pl.pallas_call(
        functools.partial(_conv_kernel, tl=tl),
        grid=(b, l // tl),
        in_specs=[
            pl.BlockSpec((None, tl, d), lambda bi, i: (bi, i, 0)),
            pl.BlockSpec((None, CONV_HALO, d), lambda bi, i: (bi, jnp.maximum(i * hb - 1, 0), 0)),
            pl.BlockSpec((CONV_HALO, d), lambda bi, i: (0, 0)),
            pl.BlockSpec((None, tl, d), lambda bi, i: (bi, i, 0)),
            pl.BlockSpec((CONV_HALO, d), lambda bi, i: (0, 0)),
            pl.BlockSpec((1, d), lambda bi, i: (0, 0)),
            pl.BlockSpec((1, d), lambda bi, i: (0, 0)),
            pl.BlockSpec((1, d), lambda bi, i: (0, 0)),
            pl.BlockSpec((d, d), lambda bi, i: (0, 0)),
            pl.BlockSpec((1, d), lambda bi, i: (0, 0)),
        ],
        out_specs=pl.BlockSpec((None, tl, d), lambda bi, i: (bi, i, 0)),
        out_shape=jax.ShapeDtypeStruct((b, l, d), F32),
        scratch_shapes=[
            pltpu.VMEM((tl + CONV_HALO, d), F32),
            pltpu.VMEM((8, tl + CONV_HALO, CONV_LANES), F32),
            pltpu.VMEM((tl, d), F32),
        ],
        compiler_params=_params("parallel", "arbitrary"),
        name="conv_block",
    )(u, u, halo0, h, dw_w, dw_b, ln_g, ln_b, w2, b2)


def _ffn_kernel(h_ref, g_ref, wg_ref, wu_ref, wd_ref, o_ref, hn_ref):
    f = pl.program_id(1)

    @pl.when(f == 0)
    def _():
        hn_ref[...] = _rms(h_ref[...], g_ref[...]).astype(BF16)

    hn = hn_ref[...]
    a = _dot(hn, wg_ref[...])
    u = _dot(hn, wu_ref[...])
    act = (a * _sigmoid(a) * u).astype(BF16)
    p = _dot(act, wd_ref[...])

    @pl.when(f == 0)
    def _():
        o_ref[...] = h_ref[...] + p

    @pl.when(f > 0)
    def _():
        o_ref[...] += p


def ffn_block(h, g, w_gate, w_up, w_down, *, tm, tf):
    m, d = h.shape
    ff = w_gate.shape[1]
    return pl.pallas_call(
        _ffn_kernel,
        grid=(m // tm, ff // tf),
        in_specs=[
            pl.BlockSpec((tm, d), lambda i, f: (i, 0)),
            pl.BlockSpec((1, d), lambda i, f: (0, 0)),
            pl.BlockSpec((d, tf), lambda i, f: (0, f)),
            pl.BlockSpec((d, tf), lambda i, f: (0, f)),
            pl.BlockSpec((tf, d), lambda i, f: (f, 0)),
        ],
        out_specs=pl.BlockSpec((tm, d), lambda i, f: (i, 0)),
        out_shape=jax.ShapeDtypeStruct((m, d), F32),
        scratch_shapes=[pltpu.VMEM((tm, d), BF16)],
        compiler_params=_params("parallel", "arbitrary"),
        name="ffn_block",
    )(h, g, w_gate, w_up, w_down)


ATTN_TILE = 256
META_KEYS = 128


def _attn_kernel(q_ref, k_ref, v_ref, km_ref, vm_ref, o_ref, *, seq):
    t = ATTN_TILE
    row = lax.broadcasted_iota(jnp.int32, (t, t), 0)
    col = lax.broadcasted_iota(jnp.int32, (t, t), 1)
    later = (row > col).astype(BF16)
    later_m = later[:META_KEYS, :META_KEYS]
    causal = col < row
    meta_ok = lax.broadcasted_iota(jnp.int32, (t, META_KEYS), 1) < N_META

    def tile(qt, kt, vt, later_mat, mask, run, acc):
        z = lax.dot_general(qt, kt, (((1,), (1,)), ((), ())), preferred_element_type=F32)
        ls = -(jnp.maximum(z, 0.0) + jnp.log1p(jnp.exp(-jnp.abs(z))))
        if mask is not None:
            ls = jnp.where(mask, ls, 0.0)
        hi = ls.astype(BF16)
        lo = (ls - hi.astype(F32)).astype(BF16)
        between = _dot(hi, later_mat) + _dot(lo, later_mat)
        w = jnp.exp(z + ls + between + run)
        if mask is not None:
            w = jnp.where(mask, w, 0.0)
        acc = acc + _dot(w.astype(BF16), vt)
        run = run + jnp.sum(ls, axis=-1, keepdims=True)
        return run, acc

    def q_loop(i, carry):
        q0 = pl.multiple_of(i * t, t)
        qt = q_ref[pl.ds(q0, t), :]
        run = jnp.zeros((t, 1), F32)
        acc = jnp.zeros((t, HEAD_DIM), F32)
        run, acc = tile(qt, k_ref[pl.ds(q0, t), :], v_ref[pl.ds(q0, t), :], later, causal, run, acc)

        def kv_loop(jj, c):
            k0 = pl.multiple_of((i - 1 - jj) * t, t)
            return tile(qt, k_ref[pl.ds(k0, t), :], v_ref[pl.ds(k0, t), :], later, None, c[0], c[1])

        run, acc = lax.fori_loop(0, i, kv_loop, (run, acc))
        run, acc = tile(qt, km_ref[...], vm_ref[...], later_m, meta_ok, run, acc)
        o_ref[pl.ds(q0, t), :] = acc.astype(o_ref.dtype)
        return carry

    lax.fori_loop(0, seq // t, q_loop, 0)


def attention(q, kv, kv_meta, *, batch, seq):
    h, hd = N_HEADS, HEAD_DIM
    return pl.pallas_call(
        functools.partial(_attn_kernel, seq=seq),
        grid=(batch, h),
        in_specs=[
            pl.BlockSpec((seq, hd), lambda b, hh: (b, hh)),
            pl.BlockSpec((seq, hd), lambda b, hh: (b, hh)),
            pl.BlockSpec((seq, hd), lambda b, hh: (b, hh + h)),
            pl.BlockSpec((META_KEYS, hd), lambda b, hh: (0, hh)),
            pl.BlockSpec((META_KEYS, hd), lambda b, hh: (0, hh + h)),
        ],
        out_specs=pl.BlockSpec((seq, hd), lambda b, hh: (b, hh)),
        out_shape=jax.ShapeDtypeStruct((batch * seq, h * hd), BF16),
        compiler_params=_params("parallel", "parallel"),
        name="attention",
    )(q, kv, kv, kv_meta, kv_meta)


def _oproj_router_kernel(o_ref, wo_ref, h_ref, g_ref, wr_ref, h3_ref, hn_ref, idx_ref, gate_ref):
    h3 = h_ref[...] + _dot(o_ref[...], wo_ref[...])
    h3_ref[...] = h3
    hn = _rms(h3, g_ref[...])
    hn_ref[...] = hn
    logits = jnp.dot(hn, wr_ref[...], preferred_element_type=F32, precision=lax.Precision.HIGHEST)
    e = lax.broadcasted_iota(jnp.int32, logits.shape, 1)
    m1 = jnp.max(logits, axis=-1, keepdims=True)
    i1 = jnp.min(jnp.where(logits == m1, e, N_EXPERTS), axis=-1, keepdims=True)
    rest = jnp.where(e == i1, -jnp.inf, logits)
    m2 = jnp.max(rest, axis=-1, keepdims=True)
    i2 = jnp.min(jnp.where(rest == m2, e, N_EXPERTS), axis=-1, keepdims=True)
    p = jnp.exp(m2 - m1)
    g1 = 1.0 / (1.0 + p)
    k = lax.broadcasted_iota(jnp.int32, idx_ref.shape, 1)
    idx_ref[...] = jnp.where(k == 0, i1, i2)
    gate_ref[...] = jnp.where(k == 0, g1, p * g1)


def oproj_router(o, wo, h, g, w_router, *, tm):
    m, d = h.shape
    return pl.pallas_call(
        _oproj_router_kernel,
        grid=(m // tm,),
        in_specs=[
            pl.BlockSpec((tm, d), lambda i: (i, 0)),
            pl.BlockSpec((d, d), lambda i: (0, 0)),
            pl.BlockSpec((tm, d), lambda i: (i, 0)),
            pl.BlockSpec((1, d), lambda i: (0, 0)),
            pl.BlockSpec((d, N_EXPERTS), lambda i: (0, 0)),
        ],
        out_specs=[
            pl.BlockSpec((tm, d), lambda i: (i, 0)),
            pl.BlockSpec((tm, d), lambda i: (i, 0)),
            pl.BlockSpec((tm, TOP_K), lambda i: (i, 0)),
            pl.BlockSpec((tm, TOP_K), lambda i: (i, 0)),
        ],
        out_shape=[
            jax.ShapeDtypeStruct((m, d), F32),
            jax.ShapeDtypeStruct((m, d), F32),
            jax.ShapeDtypeStruct((m, TOP_K), jnp.int32),
            jax.ShapeDtypeStruct((m, TOP_K), F32),
        ],
        compiler_params=_params("parallel"),
        name="oproj_router",
    )(o, wo, h, g, w_router)


def _moe_kernel(te_ref, nu_ref, nv_ref, src_ref, dst_ref,
                hn_hbm, gate_ref, wg_ref, wu_ref, wd_ref,
                y_hbm,
                xg_ref, xb_ref, acc_ref, gsem, ssem, *, tm, nf):
    i = pl.program_id(0)
    f = pl.program_id(1)
    base = i * tm

    def row_in(r, tok):
        return pltpu.make_async_copy(hn_hbm.at[pl.ds(tok, 1), :], xg_ref.at[pl.ds(r, 1), :], gsem)

    def row_out(r, dst):
        return pltpu.make_async_copy(acc_ref.at[pl.ds(r, 1), :], y_hbm.at[pl.ds(dst, 1), :], ssem)

    @pl.when(i < nu_ref[0])
    def _():
        @pl.when(f == 0)
        def _():
            def issue(r, c):
                row_in(r, src_ref[base + r]).start()
                return c

            lax.fori_loop(0, tm, issue, 0)

            def wait(r, c):
                row_in(r, 0).wait()
                return c

            lax.fori_loop(0, tm, wait, 0)
            xb_ref[...] = xg_ref[...].astype(BF16)

        x = xb_ref[...]
        a = _dot(x, wg_ref[...])
        u = _dot(x, wu_ref[...])
        act = (a * _sigmoid(a) * u).astype(BF16)
        p = _dot(act, wd_ref[...])

        @pl.when(f == 0)
        def _():
            acc_ref[...] = p

        @pl.when(f > 0)
        def _():
            acc_ref[...] += p

        @pl.when(f == nf - 1)
        def _():
            acc_ref[...] = acc_ref[...] * gate_ref[...]
            nv = nv_ref[i]

            def issue(r, c):
                row_out(r, dst_ref[base + r]).start()
                return c

            lax.fori_loop(0, nv, issue, 0)

            def wait(r, c):
                row_out(r, 0).wait()
                return c

            lax.fori_loop(0, nv, wait, 0)


def moe_experts(hn, idx, gates, w_gate, w_up, w_down, *, tm, tf):
    t, d = hn.shape
    ff = w_gate.shape[2]
    nf = ff // tf
    rows = TOP_K * t
    n_tiles = rows // tm + N_EXPERTS
    padded_rows = n_tiles * tm

    e_flat = idx.reshape(rows)
    onehot = (e_flat[:, None] == jnp.arange(N_EXPERTS, dtype=jnp.int32)[None, :]).astype(jnp.int32)
    csum = jnp.cumsum(onehot, axis=0)
    rank = jnp.sum((csum - onehot) * onehot, axis=1)
    counts = csum[-1]
    padded = ((counts + tm - 1) // tm) * tm
    gend = jnp.cumsum(padded)
    gstart = gend - padded
    dest = gstart[e_flat] + rank
    flat = jnp.arange(rows, dtype=jnp.int32)
    tok = flat // TOP_K
    slot = flat % TOP_K
    src_tok = jnp.zeros((padded_rows,), jnp.int32).at[dest].set(tok)
    dst_row = jnp.zeros((padded_rows,), jnp.int32).at[dest].set(slot * t + tok)
    gate_sorted = jnp.zeros((padded_rows,), F32).at[dest].set(gates.reshape(rows))[:, None]
    n_used = (gend[-1] // tm).astype(jnp.int32)
    tile_start = jnp.arange(n_tiles, dtype=jnp.int32) * tm
    tile_start = jnp.minimum(tile_start, (n_used - 1) * tm)
    tile_e = jnp.sum((tile_start[:, None] >= gend[None, :]).astype(jnp.int32), axis=1)
    n_valid = jnp.clip(gstart[tile_e] + counts[tile_e] - tile_start, 0, tm).astype(jnp.int32)

    def wmap_in(i, f, te, nu, nv, src, dst):
        return (te[i], 0, jnp.where(i < nu[0], f, nf - 1))

    def wmap_down(i, f, te, nu, nv, src, dst):
        return (te[i], jnp.where(i < nu[0], f, nf - 1), 0)

    grid_spec = pltpu.PrefetchScalarGridSpec(
        num_scalar_prefetch=5,
        grid=(n_tiles, nf),
        in_specs=[
            pl.BlockSpec(memory_space=pl.ANY),
            pl.BlockSpec((tm, 1), lambda i, f, *_: (i, 0)),
            pl.BlockSpec((None, d, tf), wmap_in),
            pl.BlockSpec((None, d, tf), wmap_in),
            pl.BlockSpec((None, tf, d), wmap_down),
        ],
        out_specs=pl.BlockSpec(memory_space=pl.ANY),
        scratch_shapes=[
            pltpu.VMEM((tm, d), F32),
            pltpu.VMEM((tm, d), BF16),
            pltpu.VMEM((tm, d), F32),
            pltpu.SemaphoreType.DMA(()),
            pltpu.SemaphoreType.DMA(()),
        ],
    )
    return pl.pallas_call(
        functools.partial(_moe_kernel, tm=tm, nf=nf),
        grid_spec=grid_spec,
        out_shape=jax.ShapeDtypeStruct((rows, d), F32),
        compiler_params=_params("arbitrary", "arbitrary"),
        name="moe_experts",
    )(tile_e.astype(jnp.int32), n_used.reshape(1), n_valid, src_tok, dst_row,
      hn, gate_sorted, w_gate, w_up, w_down)


def _final_kernel(h_ref, y0_ref, y1_ref, g_ref, o_ref):
    o_ref[...] = _rms(h_ref[...] + y0_ref[...] + y1_ref[...], g_ref[...])


def final_norm(h, y, g, *, tm):
    m, d = h.shape
    nb = m // tm
    return pl.pallas_call(
        _final_kernel,
        grid=(nb,),
        in_specs=[
            pl.BlockSpec((tm, d), lambda i: (i, 0)),
            pl.BlockSpec((tm, d), lambda i: (i, 0)),
            pl.BlockSpec((tm, d), lambda i: (i + nb, 0)),
            pl.BlockSpec((1, d), lambda i: (0, 0)),
        ],
        out_specs=pl.BlockSpec((tm, d), lambda i: (i, 0)),
        out_shape=jax.ShapeDtypeStruct((m, d), F32),
        compiler_params=_params("parallel"),
        name="final_norm",
    )(h, y, y, g)


def kernel(x, meta_tokens, mix_norm_g, ffn_norm_g, conv_pw1_w, conv_pw1_b, conv_dw_w, conv_dw_b,
           conv_ln_g, conv_ln_b, conv_pw2_w, conv_pw2_b, kv_norm_g, w_kv, w_q, w_o, ffn_w_gate,
           ffn_w_up, ffn_w_down, moe_router, moe_w_gate, moe_w_up, moe_w_down, final_norm_g):
    batch, seq, d = x.shape
    t = batch * seq
    row = lambda v: v.reshape(1, -1).astype(F32)

    pw1 = conv_pw1_w[0].astype(BF16)
    pw1_b = row(conv_pw1_b[0])
    dw_w = jnp.pad(conv_dw_w[0], ((0, CONV_HALO - CONV_WIDTH), (0, 0)))
    pw2 = conv_pw2_w[0].astype(BF16)
    wkv = w_kv.astype(BF16)
    wq = w_q[0].astype(BF16)
    wo = w_o[0].astype(BF16)
    fg, fu, fd = ffn_w_gate[0].astype(BF16), ffn_w_up[0].astype(BF16), ffn_w_down[0].astype(BF16)
    mg, mu, md = moe_w_gate[0].astype(BF16), moe_w_up[0].astype(BF16), moe_w_down[0].astype(BF16)

    def layer0(h, halo0, *, b, l, tm, tl):
        u = rms_glu(h, row(mix_norm_g[0]), pw1, pw1_b, tm=tm, tn=512)
        h1 = conv_block(u.reshape(b, l, d), halo0, h.reshape(b, l, d), dw_w, row(conv_dw_b[0]),
                        row(conv_ln_g[0]), row(conv_ln_b[0]), pw2, row(conv_pw2_b[0]), tl=tl)
        h2 = ffn_block(h1.reshape(b * l, d), row(ffn_norm_g[0]), fg, fu, fd, tm=tm, tf=512)
        return u, h2

    hm = jnp.pad(meta_tokens.astype(F32), ((0, META_PAD - N_META), (0, 0)))
    um, hm2 = layer0(hm, jnp.zeros((CONV_HALO, d), BF16), b=1, l=META_PAD, tm=META_PAD, tl=META_PAD)
    kv_meta = rms_proj(hm2, row(kv_norm_g), wkv, tm=META_PAD, tn=512)
    kv_meta = jnp.pad(kv_meta[:N_META], ((0, META_KEYS - N_META), (0, 0)))

    halo0 = jnp.concatenate([jnp.zeros((CONV_HALO - N_META, d), BF16), um[:N_META]], axis=0)
    _, h2 = layer0(x.reshape(t, d), halo0, b=batch, l=seq, tm=512, tl=256)

    kv = rms_proj(h2, row(kv_norm_g), wkv, tm=1024, tn=512)
    q = rms_proj(h2, row(mix_norm_g[1]), wq, tm=1024, tn=512, out_scale=HEAD_DIM ** -0.5)
    o = attention(q, kv, kv_meta, batch=batch, seq=seq)
    h3, hn3, idx, gates = oproj_router(o, wo, h2, row(ffn_norm_g[1]), moe_router[0].astype(F32), tm=256)
    y = moe_experts(hn3, idx, gates, mg, mu, md, tm=512, tf=512)
    out = final_norm(h3, y, row(final_norm_g), tm=256)
    return out.reshape(batch, seq, d)
```

```python
import functools
import math

import jax
import jax.numpy as jnp
from jax import lax
from jax.experimental import pallas as pl
from jax.experimental.pallas import tpu as pltpu

D_MODEL = 2048
N_META = 16
META_PAD = 32
N_HEADS = 16
HEAD_DIM = 128
D_ATTN = N_HEADS * HEAD_DIM
CONV_WIDTH = 31
CONV_HALO = 32
D_FF = 5632
N_EXPERTS = 8
TOP_K = 2
RMS_EPS = 1e-6
LN_EPS = 1e-5

VMEM_LIMIT_BYTES = 56 * 1024 * 1024
F32 = jnp.float32
BF16 = jnp.bfloat16
LOG2_E = math.log2(math.e)


def _dot(a, b):
    return jnp.dot(a, b, preferred_element_type=F32)


def _params(*sem):
    return pltpu.CompilerParams(dimension_semantics=sem, vmem_limit_bytes=VMEM_LIMIT_BYTES)


def _resident(shape, index_map):
    return pl.BlockSpec(shape, index_map, pipeline_mode=pl.Buffered(1))


def _rms(h, g):
    var = jnp.mean(h * h, axis=-1, keepdims=True)
    return h * lax.rsqrt(var + RMS_EPS) * g


def _sigmoid(x):
    return 1.0 / (1.0 + jnp.exp(-x))


def _split_bf16(x):
    hi = x.astype(BF16)
    lo = (x - hi.astype(F32)).astype(BF16)
    return hi, lo


def _rms_glu_kernel(h_ref, g_ref, wa_ref, wg_ref, ba_ref, bg_ref, o_ref, hn_ref):
    @pl.when(pl.program_id(1) == 0)
    def _():
        hn_ref[...] = _rms(h_ref[...], g_ref[...]).astype(BF16)

    hn = hn_ref[...]
    a = _dot(hn, wa_ref[...]) + ba_ref[...]
    g = _dot(hn, wg_ref[...]) + bg_ref[...]
    o_ref[...] = (a * _sigmoid(g)).astype(o_ref.dtype)


def rms_glu(h, g, w, b, *, tm, tn):
    m, d = h.shape
    nb = d // tn
    return pl.pallas_call(
        _rms_glu_kernel,
        grid=(m // tm, nb),
        in_specs=[
            pl.BlockSpec((tm, d), lambda i, n: (i, 0)),
            pl.BlockSpec((1, d), lambda i, n: (0, 0)),
            pl.BlockSpec((d, tn), lambda i, n: (0, n)),
            pl.BlockSpec((d, tn), lambda i, n: (0, n + nb)),
            pl.BlockSpec((1, tn), lambda i, n: (0, n)),
            pl.BlockSpec((1, tn), lambda i, n: (0, n + nb)),
        ],
        out_specs=pl.BlockSpec((tm, tn), lambda i, n: (i, n)),
        out_shape=jax.ShapeDtypeStruct((m, d), BF16),
        scratch_shapes=[pltpu.VMEM((tm, d), BF16)],
        compiler_params=_params("parallel", "arbitrary"),
        name="rms_glu",
    )(h, g, w, w, b, b)


def _rms_proj2_kernel(h_ref, g_ref, w_ref, o_ref, hn_ref, *, n_first, second_scale):
    n = pl.program_id(1)

    @pl.when(n == 0)
    def _():
        h = h_ref[...]
        y = h * lax.rsqrt(jnp.mean(h * h, axis=-1, keepdims=True) + RMS_EPS)
        hn_ref[0] = (y * g_ref[0:1, :]).astype(BF16)
        hn_ref[1] = (y * g_ref[1:2, :]).astype(BF16)

    second = n >= n_first
    r = _dot(hn_ref[second.astype(jnp.int32)], w_ref[...])
    o_ref[...] = (r * jnp.where(second, second_scale, 1.0)).astype(o_ref.dtype)


def rms_proj2(h, g2, w, *, n_cols, n_first_cols, second_scale, tm, tn):
    m, d = h.shape
    return pl.pallas_call(
        functools.partial(_rms_proj2_kernel, n_first=n_first_cols // tn, second_scale=second_scale),
        grid=(m // tm, n_cols // tn),
        in_specs=[
            pl.BlockSpec((tm, d), lambda i, j: (i, 0)),
            pl.BlockSpec((2, d), lambda i, j: (0, 0)),
            pl.BlockSpec((d, tn), lambda i, j: (0, j)),
        ],
        out_specs=pl.BlockSpec((tm, tn), lambda i, j: (i, j)),
        out_shape=jax.ShapeDtypeStruct((m, n_cols), BF16),
        scratch_shapes=[pltpu.VMEM((2, tm, d), BF16)],
        compiler_params=_params("parallel", "arbitrary"),
        name="rms_proj2",
    )(h, g2, w)


CONV_LANES = 256
CONV_ROWS = 32


def _conv_kernel(u_ref, uh_ref, halo0_ref, h_ref, dw_ref, dwb_ref, lng_ref, lnb_ref,
                 w2_ref, b2_ref, o_ref, xp_ref, sh_ref, cv_ref, *, tl):
    i = pl.program_id(1)
    d = u_ref.shape[-1]

    @pl.when(i == 0)
    def _():
        xp_ref[0:CONV_HALO, :] = halo0_ref[...].astype(F32)

    @pl.when(i > 0)
    def _():
        xp_ref[0:CONV_HALO, :] = uh_ref[...].astype(F32)

    xp_ref[CONV_HALO:CONV_HALO + tl, :] = u_ref[...].astype(F32)

    n_sh = tl + CONV_HALO - 8

    def lane_block(cb, carry):
        c0 = pl.multiple_of(cb * CONV_LANES, CONV_LANES)
        for r in range(1, 8):
            sh_ref[r, 0:n_sh, :] = xp_ref[pl.ds(r, n_sh), pl.ds(c0, CONV_LANES)]

        def row_chunk(rc, carry2):
            r0 = pl.multiple_of(rc * CONV_ROWS, CONV_ROWS)
            acc = jnp.zeros((CONV_ROWS, CONV_LANES), F32)
            for w in range(CONV_WIDTH):
                a, r = divmod(CONV_HALO - (CONV_WIDTH - 1) + w, 8)
                if r == 0:
                    xs = xp_ref[pl.ds(r0 + 8 * a, CONV_ROWS), pl.ds(c0, CONV_LANES)]
                else:
                    xs = sh_ref[r, pl.ds(r0 + 8 * a, CONV_ROWS), :]
                acc = acc + xs * dw_ref[pl.ds(w, 1), pl.ds(c0, CONV_LANES)]
            cv_ref[pl.ds(r0, CONV_ROWS), pl.ds(c0, CONV_LANES)] = acc + dwb_ref[:, pl.ds(c0, CONV_LANES)]
            return carry2

        lax.fori_loop(0, tl // CONV_ROWS, row_chunk, 0)
        return carry

    lax.fori_loop(0, d // CONV_LANES, lane_block, 0)

    c = cv_ref[...]
    mu = jnp.mean(c, axis=-1, keepdims=True)
    cc = c - mu
    var = jnp.mean(cc * cc, axis=-1, keepdims=True)
    y = cc * lax.rsqrt(var + LN_EPS) * lng_ref[...] + lnb_ref[...]
    s = (y * _sigmoid(y)).astype(BF16)
    o_ref[...] = h_ref[...] + _dot(s, w2_ref[...]) + b2_ref[...]


def conv_block(u, halo0, h, dw_w, dw_b, ln_g, ln_b, w2, b2, *, tl):
    b, l, d = u.shape
    hb = tl // CONV_HALO
    return pl.pallas_call(
        functools.partial(_conv_kernel, tl=tl),
        grid=(b, l // tl),
        in_specs=[
            pl.BlockSpec((None, tl, d), lambda bi, i: (bi, i, 0)),
            pl.BlockSpec((None, CONV_HALO, d), lambda bi, i: (bi, jnp.maximum(i * hb - 1, 0), 0)),
            _resident((CONV_HALO, d), lambda bi, i: (0, 0)),
            pl.BlockSpec((None, tl, d), lambda bi, i: (bi, i, 0)),
            _resident((CONV_HALO, d), lambda bi, i: (0, 0)),
            _resident((1, d), lambda bi, i: (0, 0)),
            _resident((1, d), lambda bi, i: (0, 0)),
            _resident((1, d), lambda bi, i: (0, 0)),
            _resident((d, d), lambda bi, i: (0, 0)),
            _resident((1, d), lambda bi, i: (0, 0)),
        ],
        out_specs=pl.BlockSpec((None, tl, d), lambda bi, i: (bi, i, 0)),
        out_shape=jax.ShapeDtypeStruct((b, l, d), F32),
        scratch_shapes=[
            pltpu.VMEM((tl + CONV_HALO, d), F32),
            pltpu.VMEM((8, tl + CONV_HALO, CONV_LANES), F32),
            pltpu.VMEM((tl, d), F32),
        ],
        compiler_params=_params("parallel", "arbitrary"),
        name="conv_block",
    )(u, u, halo0, h, dw_w, dw_b, ln_g, ln_b, w2, b2)


def _swiglu_partial(x, wg_ref, wu_ref, wd_ref):
    a = _dot(x, wg_ref[...])
    u = _dot(x, wu_ref[...])
    act = (a * _sigmoid(a) * u).astype(BF16)
    return _dot(act, wd_ref[...])


def _ffn_kernel(h_ref, g_ref, wg_ref, wu_ref, wd_ref, o_ref, hn_ref):
    @pl.when(pl.program_id(1) == 0)
    def _():
        h = h_ref[...]
        hn_ref[...] = _rms(h, g_ref[...]).astype(BF16)
        o_ref[...] = h

    o_ref[...] += _swiglu_partial(hn_ref[...], wg_ref, wu_ref, wd_ref)


def ffn_block(h, g, w_gate, w_up, w_down, *, tm, tf):
    m, d = h.shape
    ff = w_gate.shape[1]
    return pl.pallas_call(
        _ffn_kernel,
        grid=(m // tm, ff // tf),
        in_specs=[
            pl.BlockSpec((tm, d), lambda i, f: (i, 0)),
            pl.BlockSpec((1, d), lambda i, f: (0, 0)),
            pl.BlockSpec((d, tf), lambda i, f: (0, f)),
            pl.BlockSpec((d, tf), lambda i, f: (0, f)),
            pl.BlockSpec((tf, d), lambda i, f: (f, 0)),
        ],
        out_specs=pl.BlockSpec((tm, d), lambda i, f: (i, 0)),
        out_shape=jax.ShapeDtypeStruct((m, d), F32),
        scratch_shapes=[pltpu.VMEM((tm, d), BF16)],
        compiler_params=_params("parallel", "arbitrary"),
        name="ffn_block",
    )(h, g, w_gate, w_up, w_down)


ATTN_TILE = 256
ATTN_HEADS = 4
META_KEYS = 128
Q_PRESCALE = -(HEAD_DIM ** -0.5) * LOG2_E


def _attn_kernel(q_ref, k_ref, v_ref, km_ref, vm_ref, o_ref, *, seq):
    t = ATTN_TILE
    row = lax.broadcasted_iota(jnp.int32, (t, t), 0)
    col = lax.broadcasted_iota(jnp.int32, (t, t), 1)
    later = (row > col).astype(BF16)
    later2 = jnp.concatenate([later, later], axis=0)
    later_m = later[:META_KEYS, :META_KEYS]
    later_m2 = jnp.concatenate([later_m, later_m], axis=0)
    causal = col < row
    meta_ok = lax.broadcasted_iota(jnp.int32, (t, META_KEYS), 1) < N_META

    heads = [slice(hh * HEAD_DIM, (hh + 1) * HEAD_DIM) for hh in range(ATTN_HEADS)]

    def scores(qts, kts, later_mat, mask):
        ys = [lax.dot_general(qt, kt, (((1,), (1,)), ((), ())), preferred_element_type=F32)
              for qt, kt in zip(qts, kts)]
        lss = []
        for y in ys:
            e = jnp.exp2(-jnp.abs(y))
            ls = jnp.minimum(y, 0.0) - jnp.log(1.0 + e) * LOG2_E
            if mask is not None:
                ls = jnp.where(mask, ls, 0.0)
            lss.append(ls)
        betweens = [_dot(jnp.concatenate(_split_bf16(ls), axis=1), later_mat) for ls in lss]
        sums, logws = [], []
        for y, ls, between in zip(ys, lss, betweens):
            logw = ls - y + between
            if mask is not None:
                logw = jnp.where(mask, logw, -jnp.inf)
            sums.append(jnp.sum(ls, axis=-1, keepdims=True))
            logws.append(logw)
        return sums, logws

    def weighted_values(logws, runs, vts):
        return [_dot(jnp.exp2(logw + run).astype(BF16), vt) for logw, run, vt in zip(logws, runs, vts)]

    def q_loop(i, carry):
        q0 = pl.multiple_of(i * t, t)
        qts = [q_ref[pl.ds(q0, t), sl] for sl in heads]

        def kv(k0):
            return [k_ref[pl.ds(k0, t), sl] for sl in heads], [v_ref[pl.ds(k0, t), sl] for sl in heads]

        kts, vts = kv(q0)
        runs, logws = scores(qts, kts, later2, causal)
        _, logws_meta = scores(qts, [km_ref[:, sl] for sl in heads], later_m2, meta_ok)
        accs = weighted_values(logws, [0.0] * ATTN_HEADS, vts)

        def kv_loop(jj, c):
            runs, accs = c
            kts, vts = kv(pl.multiple_of((i - 1 - jj) * t, t))
            sums, logws = scores(qts, kts, later2, None)
            pvs = weighted_values(logws, runs, vts)
            return [run + s for run, s in zip(runs, sums)], [acc + pv for acc, pv in zip(accs, pvs)]

        runs, accs = lax.fori_loop(0, i, kv_loop, (runs, accs))
        pvs = weighted_values(logws_meta, runs, [vm_ref[:, sl] for sl in heads])
        for sl, acc, pv in zip(heads, accs, pvs):
            o_ref[pl.ds(q0, t), sl] = (acc + pv).astype(o_ref.dtype)
        return carry

    lax.fori_loop(0, seq // t, q_loop, 0)


def attention(qkv, kv_meta, *, batch, seq):
    w = ATTN_HEADS * HEAD_DIM
    nb = D_ATTN // w
    return pl.pallas_call(
        functools.partial(_attn_kernel, seq=seq),
        grid=(batch, nb),
        in_specs=[
            pl.BlockSpec((seq, w), lambda b, hp: (b, 2 * nb + hp)),
            pl.BlockSpec((seq, w), lambda b, hp: (b, hp)),
            pl.BlockSpec((seq, w), lambda b, hp: (b, nb + hp)),
            pl.BlockSpec((META_KEYS, w), lambda b, hp: (0, hp)),
            pl.BlockSpec((META_KEYS, w), lambda b, hp: (0, nb + hp)),
        ],
        out_specs=pl.BlockSpec((seq, w), lambda b, hp: (b, hp)),
        out_shape=jax.ShapeDtypeStruct((batch * seq, D_ATTN), BF16),
        compiler_params=_params("parallel", "parallel"),
        name="attention",
    )(qkv, qkv, qkv, kv_meta, kv_meta)


def _oproj_router_kernel(o_ref, wo_ref, h_ref, g_ref, wr_ref, h3_ref, hn_ref, idx_ref, gate_ref):
    h3 = h_ref[...] + _dot(o_ref[...], wo_ref[...])
    h3_ref[...] = h3
    hn = _rms(h3, g_ref[...])
    hn_ref[...] = hn
    hn_hi, hn_lo = _split_bf16(hn)
    wr_hi, wr_lo = _split_bf16(wr_ref[...])
    logits = _dot(hn_hi, wr_hi) + (_dot(hn_hi, wr_lo) + _dot(hn_lo, wr_hi))
    e = lax.broadcasted_iota(jnp.int32, logits.shape, 1)
    m1 = jnp.max(logits, axis=-1, keepdims=True)
    i1 = jnp.min(jnp.where(logits == m1, e, N_EXPERTS), axis=-1, keepdims=True)
    rest = jnp.where(e == i1, -jnp.inf, logits)
    m2 = jnp.max(rest, axis=-1, keepdims=True)
    i2 = jnp.min(jnp.where(rest == m2, e, N_EXPERTS), axis=-1, keepdims=True)
    p = jnp.exp(m2 - m1)
    g1 = 1.0 / (1.0 + p)
    k = lax.broadcasted_iota(jnp.int32, idx_ref.shape, 1)
    idx_ref[...] = jnp.where(k == 0, i1, i2)
    gate_ref[...] = jnp.where(k == 0, g1, p * g1)


def oproj_router(o, wo, h, g, w_router, *, tm):
    m, d = h.shape
    return pl.pallas_call(
        _oproj_router_kernel,
        grid=(m // tm,),
        in_specs=[
            pl.BlockSpec((tm, d), lambda i: (i, 0)),
            _resident((d, d), lambda i: (0, 0)),
            pl.BlockSpec((tm, d), lambda i: (i, 0)),
            _resident((1, d), lambda i: (0, 0)),
            _resident((d, N_EXPERTS), lambda i: (0, 0)),
        ],
        out_specs=[
            pl.BlockSpec((tm, d), lambda i: (i, 0)),
            pl.BlockSpec((tm, d), lambda i: (i, 0)),
            pl.BlockSpec((tm, TOP_K), lambda i: (i, 0)),
            pl.BlockSpec((tm, TOP_K), lambda i: (i, 0)),
        ],
        out_shape=[
            jax.ShapeDtypeStruct((m, d), F32),
            jax.ShapeDtypeStruct((m, d), F32),
            jax.ShapeDtypeStruct((m, TOP_K), jnp.int32),
            jax.ShapeDtypeStruct((m, TOP_K), F32),
        ],
        compiler_params=_params("parallel"),
        name="oproj_router",
    )(o, wo, h, g, w_router)


ROW_DMA_UNROLL = 8


def _dispatch_kernel(dest_ref, hn_ref, x0_hbm, x_hbm, sem, *, tm):
    del x0_hbm
    base = pl.program_id(0) * (tm * TOP_K)

    def row(r, k, dst):
        return pltpu.make_async_copy(hn_ref.at[pl.ds(r, 1), :], x_hbm.at[pl.ds(dst, 1), :], sem)

    def issue(r, c):
        for k in range(TOP_K):
            row(r, k, dest_ref[base + TOP_K * r + k]).start()
        return c

    lax.fori_loop(0, tm, issue, 0, unroll=ROW_DMA_UNROLL)

    def wait(r, c):
        for k in range(TOP_K):
            row(r, k, 0).wait()
        return c

    lax.fori_loop(0, tm, wait, 0, unroll=ROW_DMA_UNROLL)


def moe_dispatch(hn, dest, padded_rows, *, tm):
    t, d = hn.shape
    grid_spec = pltpu.PrefetchScalarGridSpec(
        num_scalar_prefetch=1,
        grid=(t // tm,),
        in_specs=[
            pl.BlockSpec((tm, d), lambda i, dest_ref: (i, 0)),
            pl.BlockSpec(memory_space=pl.ANY),
        ],
        out_specs=pl.BlockSpec(memory_space=pl.ANY),
        scratch_shapes=[pltpu.SemaphoreType.DMA(())],
    )
    return pl.pallas_call(
        functools.partial(_dispatch_kernel, tm=tm),
        grid_spec=grid_spec,
        out_shape=jax.ShapeDtypeStruct((padded_rows, d), F32),
        input_output_aliases={2: 0},
        compiler_params=_params("arbitrary"),
        name="moe_dispatch",
    )(dest.reshape(-1), hn, jnp.zeros((padded_rows, d), F32))


def _moe_kernel(te_ref, nu_ref, x_ref, wg_ref, wu_ref, wd_ref, o_ref, xb_ref):
    i = pl.program_id(0)
    f = pl.program_id(1)
    used = i < nu_ref[0]

    @pl.when(f == 0)
    def _():
        xb_ref[...] = x_ref[...].astype(BF16)
        o_ref[...] = jnp.zeros_like(o_ref)

    @pl.when(used)
    def _():
        o_ref[...] += _swiglu_partial(xb_ref[...], wg_ref, wu_ref, wd_ref)


def moe_experts(x, tile_e, n_used, w_gate, w_up, w_down, *, tm, tf):
    p, d = x.shape
    nf = w_gate.shape[2] // tf

    def wmap_in(i, f, te, nu):
        return (te[i], 0, jnp.where(i < nu[0], f, nf - 1))

    def wmap_down(i, f, te, nu):
        return (te[i], jnp.where(i < nu[0], f, nf - 1), 0)

    grid_spec = pltpu.PrefetchScalarGridSpec(
        num_scalar_prefetch=2,
        grid=(p // tm, nf),
        in_specs=[
            pl.BlockSpec((tm, d), lambda i, f, te, nu: (i, 0)),
            pl.BlockSpec((None, d, tf), wmap_in),
            pl.BlockSpec((None, d, tf), wmap_in),
            pl.BlockSpec((None, tf, d), wmap_down),
        ],
        out_specs=pl.BlockSpec((tm, d), lambda i, f, te, nu: (i, 0)),
        scratch_shapes=[pltpu.VMEM((tm, d), BF16)],
    )
    return pl.pallas_call(
        _moe_kernel,
        grid_spec=grid_spec,
        out_shape=jax.ShapeDtypeStruct((p, d), F32),
        compiler_params=_params("arbitrary", "arbitrary"),
        name="moe_experts",
    )(tile_e, n_used, x, w_gate, w_up, w_down)


def _combine_kernel(dest_ref, h_ref, gate_ref, g_ref, y_hbm, o_ref, ybuf, sem, *, tm, n_steps):
    i = pl.program_id(0)

    def row(slot, r, k, src):
        return pltpu.make_async_copy(y_hbm.at[pl.ds(src, 1), :], ybuf.at[slot, k, pl.ds(r, 1), :], sem.at[slot])

    def fetch(step, slot):
        base = step * (tm * TOP_K)

        def issue(r, c):
            for k in range(TOP_K):
                row(slot, r, k, dest_ref[base + TOP_K * r + k]).start()
            return c

        lax.fori_loop(0, tm, issue, 0, unroll=ROW_DMA_UNROLL)

    @pl.when(i == 0)
    def _():
        fetch(0, 0)

    @pl.when(i + 1 < n_steps)
    def _():
        fetch(i + 1, (i + 1) % 2)

    slot = i % 2

    def wait(r, c):
        for k in range(TOP_K):
            row(slot, r, k, 0).wait()
        return c

    lax.fori_loop(0, tm, wait, 0, unroll=ROW_DMA_UNROLL)
    gate = gate_ref[...]
    y = h_ref[...] + gate[:, 0:1] * ybuf[slot, 0] + gate[:, 1:2] * ybuf[slot, 1]
    o_ref[...] = _rms(y, g_ref[...])


def moe_combine_norm(h, gates, y, dest, g, *, tm):
    t, d = h.shape
    n_steps = t // tm
    grid_spec = pltpu.PrefetchScalarGridSpec(
        num_scalar_prefetch=1,
        grid=(n_steps,),
        in_specs=[
            pl.BlockSpec((tm, d), lambda i, dest_ref: (i, 0)),
            pl.BlockSpec((tm, TOP_K), lambda i, dest_ref: (i, 0)),
            pl.BlockSpec((1, d), lambda i, dest_ref: (0, 0)),
            pl.BlockSpec(memory_space=pl.ANY),
        ],
        out_specs=pl.BlockSpec((tm, d), lambda i, dest_ref: (i, 0)),
        scratch_shapes=[pltpu.VMEM((2, TOP_K, tm, d), F32), pltpu.SemaphoreType.DMA((2,))],
    )
    return pl.pallas_call(
        functools.partial(_combine_kernel, tm=tm, n_steps=n_steps),
        grid_spec=grid_spec,
        out_shape=jax.ShapeDtypeStruct((t, d), F32),
        compiler_params=_params("arbitrary"),
        name="moe_combine_norm",
    )(dest.reshape(-1), h, gates, g, y)


def routing_tables(idx, *, tm):
    t = idx.shape[0]
    rows = TOP_K * t
    n_tiles = rows // tm + N_EXPERTS
    e_flat = idx.reshape(rows)
    onehot = (e_flat[:, None] == jnp.arange(N_EXPERTS, dtype=jnp.int32)[None, :]).astype(jnp.int32)
    csum = jnp.cumsum(onehot, axis=0)
    rank = jnp.sum((csum - onehot) * onehot, axis=1)
    counts = csum[-1]
    padded = ((counts + tm - 1) // tm) * tm
    gend = jnp.cumsum(padded)
    gstart = gend - padded
    dest = (jnp.sum(onehot * gstart[None, :], axis=1) + rank).astype(jnp.int32).reshape(t, TOP_K)
    n_used = (gend[-1] // tm).astype(jnp.int32).reshape(1)
    tile_start = jnp.minimum(jnp.arange(n_tiles, dtype=jnp.int32), n_used - 1) * tm
    tile_e = jnp.sum((tile_start[:, None] >= gend[None, :]).astype(jnp.int32), axis=1).astype(jnp.int32)
    return dest, tile_e, n_used, n_tiles * tm


def kernel(x, meta_tokens, mix_norm_g, ffn_norm_g, conv_pw1_w, conv_pw1_b, conv_dw_w, conv_dw_b,
           conv_ln_g, conv_ln_b, conv_pw2_w, conv_pw2_b, kv_norm_g, w_kv, w_q, w_o, ffn_w_gate,
           ffn_w_up, ffn_w_down, moe_router, moe_w_gate, moe_w_up, moe_w_down, final_norm_g):
    batch, seq, d = x.shape
    t = batch * seq
    row = lambda v: v.reshape(1, -1).astype(F32)

    pw1 = conv_pw1_w[0].astype(BF16)
    pw1_b = row(conv_pw1_b[0])
    dw_w = jnp.pad(conv_dw_w[0], ((0, CONV_HALO - CONV_WIDTH), (0, 0)))
    pw2 = conv_pw2_w[0].astype(BF16)
    w_kvq = jnp.concatenate([w_kv.astype(BF16), w_q[0].astype(BF16)], axis=1)
    g_kvq = jnp.stack([kv_norm_g, mix_norm_g[1]]).astype(F32)
    wo = w_o[0].astype(BF16)
    fg, fu, fd = ffn_w_gate[0].astype(BF16), ffn_w_up[0].astype(BF16), ffn_w_down[0].astype(BF16)
    mg, mu, md = moe_w_gate[0].astype(BF16), moe_w_up[0].astype(BF16), moe_w_down[0].astype(BF16)

    def layer0(h, halo0, *, b, l, tm, tl):
        u = rms_glu(h, row(mix_norm_g[0]), pw1, pw1_b, tm=tm, tn=512)
        h1 = conv_block(u.reshape(b, l, d), halo0, h.reshape(b, l, d), dw_w, row(conv_dw_b[0]),
                        row(conv_ln_g[0]), row(conv_ln_b[0]), pw2, row(conv_pw2_b[0]), tl=tl)
        h2 = ffn_block(h1.reshape(b * l, d), row(ffn_norm_g[0]), fg, fu, fd, tm=tm, tf=512)
        return u, h2

    hm = jnp.pad(meta_tokens.astype(F32), ((0, META_PAD - N_META), (0, 0)))
    um, hm2 = layer0(hm, jnp.zeros((CONV_HALO, d), BF16), b=1, l=META_PAD, tm=META_PAD, tl=META_PAD)
    kv_meta = rms_proj2(hm2, g_kvq, w_kvq, n_cols=2 * D_ATTN, n_first_cols=2 * D_ATTN, second_scale=1.0,
                        tm=META_PAD, tn=512)
    kv_meta = jnp.pad(kv_meta[:N_META], ((0, META_KEYS - N_META), (0, 0)))

    halo0 = jnp.concatenate([jnp.zeros((CONV_HALO - N_META, d), BF16), um[:N_META]], axis=0)
    _, h2 = layer0(x.reshape(t, d), halo0, b=batch, l=seq, tm=512, tl=256)

    qkv = rms_proj2(h2, g_kvq, w_kvq, n_cols=3 * D_ATTN, n_first_cols=2 * D_ATTN, second_scale=Q_PRESCALE,
                    tm=1024, tn=512)
    o = attention(qkv, kv_meta, batch=batch, seq=seq)
    h3, hn3, idx, gates = oproj_router(o, wo, h2, row(ffn_norm_g[1]), moe_router[0].astype(F32), tm=512)

    moe_tm = 512
    dest, tile_e, n_used, padded_rows = routing_tables(idx, tm=moe_tm)
    xs = moe_dispatch(hn3, dest, padded_rows, tm=512)
    ys = moe_experts(xs, tile_e, n_used, mg, mu, md, tm=moe_tm, tf=512)
    out = moe_combine_norm(h3, gates, ys, dest, row(final_norm_g), tm=256)
    return out.reshape(batch, seq, d)
```

```python
import functools
import math

import jax
import jax.numpy as jnp
from jax import lax
from jax.experimental import pallas as pl
from jax.experimental.pallas import tpu as pltpu

D_MODEL = 2048
N_META = 16
META_PAD = 32
N_HEADS = 16
HEAD_DIM = 128
D_ATTN = N_HEADS * HEAD_DIM
CONV_WIDTH = 31
CONV_HALO = 32
D_FF = 5632
N_EXPERTS = 8
TOP_K = 2
RMS_EPS = 1e-6
LN_EPS = 1e-5

VMEM_LIMIT_BYTES = 56 * 1024 * 1024
F32 = jnp.float32
BF16 = jnp.bfloat16
LOG2_E = math.log2(math.e)


def _dot(a, b):
    return jnp.dot(a, b, preferred_element_type=F32)


def _params(*sem):
    return pltpu.CompilerParams(dimension_semantics=sem, vmem_limit_bytes=VMEM_LIMIT_BYTES)


def _resident(shape, index_map):
    return pl.BlockSpec(shape, index_map, pipeline_mode=pl.Buffered(1))


def _rms(h, g):
    var = jnp.mean(h * h, axis=-1, keepdims=True)
    return h * lax.rsqrt(var + RMS_EPS) * g


def _sigmoid(x):
    return 1.0 / (1.0 + jnp.exp(-x))


def _split_bf16(x):
    hi = x.astype(BF16)
    lo = (x - hi.astype(F32)).astype(BF16)
    return hi, lo


def _rms_glu_kernel(h_ref, g_ref, wa_ref, wg_ref, ba_ref, bg_ref, o_ref, hn_ref):
    @pl.when(pl.program_id(1) == 0)
    def _():
        hn_ref[...] = _rms(h_ref[...], g_ref[...]).astype(BF16)

    hn = hn_ref[...]
    half = o_ref.shape[1] // 2
    for sl in (slice(0, half), slice(half, 2 * half)):
        a = _dot(hn, wa_ref[:, sl]) + ba_ref[:, sl]
        g = _dot(hn, wg_ref[:, sl]) + bg_ref[:, sl]
        o_ref[:, sl] = (a * _sigmoid(g)).astype(o_ref.dtype)


def rms_glu(h, g, w, b, *, tm, tn):
    m, d = h.shape
    nb = d // tn
    return pl.pallas_call(
        _rms_glu_kernel,
        grid=(m // tm, nb),
        in_specs=[
            pl.BlockSpec((tm, d), lambda i, n: (i, 0)),
            pl.BlockSpec((1, d), lambda i, n: (0, 0)),
            pl.BlockSpec((d, tn), lambda i, n: (0, n)),
            pl.BlockSpec((d, tn), lambda i, n: (0, n + nb)),
            pl.BlockSpec((1, tn), lambda i, n: (0, n)),
            pl.BlockSpec((1, tn), lambda i, n: (0, n + nb)),
        ],
        out_specs=pl.BlockSpec((tm, tn), lambda i, n: (i, n)),
        out_shape=jax.ShapeDtypeStruct((m, d), BF16),
        scratch_shapes=[pltpu.VMEM((tm, d), BF16)],
        compiler_params=_params("parallel", "arbitrary"),
        name="rms_glu",
    )(h, g, w, w, b, b)


def _rms_proj2_kernel(h_ref, g_ref, w_ref, o_ref, hn_ref, *, n_first, second_scale):
    n = pl.program_id(1)

    @pl.when(n == 0)
    def _():
        h = h_ref[...]
        y = h * lax.rsqrt(jnp.mean(h * h, axis=-1, keepdims=True) + RMS_EPS)
        hn_ref[0] = (y * g_ref[0:1, :]).astype(BF16)
        hn_ref[1] = (y * g_ref[1:2, :]).astype(BF16)

    second = n >= n_first
    r = _dot(hn_ref[second.astype(jnp.int32)], w_ref[...])
    o_ref[...] = (r * jnp.where(second, second_scale, 1.0)).astype(o_ref.dtype)


def rms_proj2(h, g2, w, *, n_cols, n_first_cols, second_scale, tm, tn):
    m, d = h.shape
    return pl.pallas_call(
        functools.partial(_rms_proj2_kernel, n_first=n_first_cols // tn, second_scale=second_scale),
        grid=(m // tm, n_cols // tn),
        in_specs=[
            pl.BlockSpec((tm, d), lambda i, j: (i, 0)),
            pl.BlockSpec((2, d), lambda i, j: (0, 0)),
            pl.BlockSpec((d, tn), lambda i, j: (0, j)),
        ],
        out_specs=pl.BlockSpec((tm, tn), lambda i, j: (i, j)),
        out_shape=jax.ShapeDtypeStruct((m, n_cols), BF16),
        scratch_shapes=[pltpu.VMEM((2, tm, d), BF16)],
        compiler_params=_params("parallel", "arbitrary"),
        name="rms_proj2",
    )(h, g2, w)


CONV_LANES = 256
CONV_ROWS = 64


def _conv_kernel(u_ref, uh_ref, halo0_ref, h_ref, dw_ref, dwb_ref, lng_ref, lnb_ref,
                 w2_ref, b2_ref, o_ref, xp_ref, sh_ref, cv_ref, *, tl):
    i = pl.program_id(1)
    d = u_ref.shape[-1]

    @pl.when(i == 0)
    def _():
        xp_ref[0:CONV_HALO, :] = halo0_ref[...].astype(F32)

    @pl.when(i > 0)
    def _():
        xp_ref[0:CONV_HALO, :] = uh_ref[...].astype(F32)

    xp_ref[CONV_HALO:CONV_HALO + tl, :] = u_ref[...].astype(F32)

    n_sh = tl + CONV_HALO - 8
    rows = min(CONV_ROWS, tl)

    def lane_block(cb, carry):
        c0 = pl.multiple_of(cb * CONV_LANES, CONV_LANES)
        for r in range(1, 8):
            sh_ref[r, 0:n_sh, :] = xp_ref[pl.ds(r, n_sh), pl.ds(c0, CONV_LANES)]

        def row_chunk(rc, carry2):
            r0 = pl.multiple_of(rc * rows, rows)
            acc = jnp.zeros((rows, CONV_LANES), F32)
            for w in range(CONV_WIDTH):
                a, r = divmod(CONV_HALO - (CONV_WIDTH - 1) + w, 8)
                if r == 0:
                    xs = xp_ref[pl.ds(r0 + 8 * a, rows), pl.ds(c0, CONV_LANES)]
                else:
                    xs = sh_ref[r, pl.ds(r0 + 8 * a, rows), :]
                acc = acc + xs * dw_ref[pl.ds(w, 1), pl.ds(c0, CONV_LANES)]
            cv_ref[pl.ds(r0, rows), pl.ds(c0, CONV_LANES)] = acc + dwb_ref[:, pl.ds(c0, CONV_LANES)]
            return carry2

        lax.fori_loop(0, tl // rows, row_chunk, 0)
        return carry

    lax.fori_loop(0, d // CONV_LANES, lane_block, 0)

    c = cv_ref[...]
    mu = jnp.mean(c, axis=-1, keepdims=True)
    cc = c - mu
    var = jnp.mean(cc * cc, axis=-1, keepdims=True)
    y = cc * lax.rsqrt(var + LN_EPS) * lng_ref[...] + lnb_ref[...]
    s = (y * _sigmoid(y)).astype(BF16)
    o_ref[...] = h_ref[...] + _dot(s, w2_ref[...]) + b2_ref[...]


def conv_block(u, halo0, h, dw_w, dw_b, ln_g, ln_b, w2, b2, *, tl):
    b, l, d = u.shape
    hb = tl // CONV_HALO
    return pl.pallas_call(
        functools.partial(_conv_kernel, tl=tl),
        grid=(b, l // tl),
        in_specs=[
            pl.BlockSpec((None, tl, d), lambda bi, i: (bi, i, 0)),
            pl.BlockSpec((None, CONV_HALO, d), lambda bi, i: (bi, jnp.maximum(i * hb - 1, 0), 0)),
            _resident((CONV_HALO, d), lambda bi, i: (0, 0)),
            pl.BlockSpec((None, tl, d), lambda bi, i: (bi, i, 0)),
            _resident((CONV_HALO, d), lambda bi, i: (0, 0)),
            _resident((1, d), lambda bi, i: (0, 0)),
            _resident((1, d), lambda bi, i: (0, 0)),
            _resident((1, d), lambda bi, i: (0, 0)),
            _resident((d, d), lambda bi, i: (0, 0)),
            _resident((1, d), lambda bi, i: (0, 0)),
        ],
        out_specs=pl.BlockSpec((None, tl, d), lambda bi, i: (bi, i, 0)),
        out_shape=jax.ShapeDtypeStruct((b, l, d), F32),
        scratch_shapes=[
            pltpu.VMEM((tl + CONV_HALO, d), F32),
            pltpu.VMEM((8, tl + CONV_HALO, CONV_LANES), F32),
            pltpu.VMEM((tl, d), F32),
        ],
        compiler_params=_params("parallel", "arbitrary"),
        name="conv_block",
    )(u, u, halo0, h, dw_w, dw_b, ln_g, ln_b, w2, b2)


def _swiglu_partial(x, wg_ref, wu_ref, wd_ref):
    a = _dot(x, wg_ref[...])
    u = _dot(x, wu_ref[...])
    act = (a * _sigmoid(a) * u).astype(BF16)
    return _dot(act, wd_ref[...])


def _ffn_kernel(h_ref, g_ref, wg_ref, wu_ref, wd_ref, o_ref, hn_ref):
    @pl.when(pl.program_id(1) == 0)
    def _():
        h = h_ref[...]
        hn_ref[...] = _rms(h, g_ref[...]).astype(BF16)
        o_ref[...] = h

    o_ref[...] += _swiglu_partial(hn_ref[...], wg_ref, wu_ref, wd_ref)


def ffn_block(h, g, w_gate, w_up, w_down, *, tm, tf):
    m, d = h.shape
    ff = w_gate.shape[1]
    return pl.pallas_call(
        _ffn_kernel,
        grid=(m // tm, ff // tf),
        in_specs=[
            pl.BlockSpec((tm, d), lambda i, f: (i, 0)),
            pl.BlockSpec((1, d), lambda i, f: (0, 0)),
            pl.BlockSpec((d, tf), lambda i, f: (0, f)),
            pl.BlockSpec((d, tf), lambda i, f: (0, f)),
            pl.BlockSpec((tf, d), lambda i, f: (f, 0)),
        ],
        out_specs=pl.BlockSpec((tm, d), lambda i, f: (i, 0)),
        out_shape=jax.ShapeDtypeStruct((m, d), F32),
        scratch_shapes=[pltpu.VMEM((tm, d), BF16)],
        compiler_params=_params("parallel", "arbitrary"),
        name="ffn_block",
    )(h, g, w_gate, w_up, w_down)


ATTN_TILE = 256
ATTN_HEADS = 4
META_KEYS = 128
Q_PRESCALE = -(HEAD_DIM ** -0.5) * LOG2_E


def _attn_kernel(q_ref, k_ref, v_ref, km_ref, vm_ref, *rest, seq, n_cast):
    cast_src = rest[:n_cast]
    o_ref = rest[n_cast]
    cast_dst = rest[n_cast + 1:2 * n_cast + 1]
    cast_in = rest[2 * n_cast + 1:3 * n_cast + 1]
    cast_out = rest[3 * n_cast + 1:4 * n_cast + 1]
    sem_in, sem_out = rest[4 * n_cast + 1:]
    t = ATTN_TILE
    nq = seq // t
    step = pl.program_id(0) * pl.num_programs(1) + pl.program_id(1)

    def chunk_rows(a, chunk):
        rows = cast_in[a].shape[1]
        return pl.ds(pl.multiple_of(chunk * rows, 16), rows)

    def cast_loads(chunk, slot):
        return [pltpu.make_async_copy(cast_src[a].at[chunk_rows(a, chunk), :], cast_in[a].at[slot], sem_in.at[slot, a])
                for a in range(n_cast)]

    def cast_stores(chunk, slot):
        return [pltpu.make_async_copy(cast_out[a].at[slot], cast_dst[a].at[chunk_rows(a, chunk), :], sem_out.at[slot, a])
                for a in range(n_cast)]

    row = lax.broadcasted_iota(jnp.int32, (t, t), 0)
    col = lax.broadcasted_iota(jnp.int32, (t, t), 1)
    later = (row > col).astype(BF16)
    later2 = jnp.concatenate([later, later], axis=0)
    later_m = later[:META_KEYS, :META_KEYS]
    later_m2 = jnp.concatenate([later_m, later_m], axis=0)
    causal = col < row
    meta_ok = lax.broadcasted_iota(jnp.int32, (t, META_KEYS), 1) < N_META

    heads = [slice(hh * HEAD_DIM, (hh + 1) * HEAD_DIM) for hh in range(ATTN_HEADS)]

    def scores(qts, kts, later_mat, mask):
        ys = [lax.dot_general(qt, kt, (((1,), (1,)), ((), ())), preferred_element_type=F32)
              for qt, kt in zip(qts, kts)]
        lss = []
        for y in ys:
            e = jnp.exp2(jnp.minimum(y, -y))
            ls = jnp.minimum(y, 0.0) - jnp.log2(1.0 + e)
            if mask is not None:
                ls = jnp.where(mask, ls, 0.0)
            lss.append(ls)
        betweens = [_dot(jnp.concatenate(_split_bf16(ls), axis=1), later_mat) for ls in lss]
        sums, logws = [], []
        for y, ls, between in zip(ys, lss, betweens):
            logw = ls - y + between
            if mask is not None:
                logw = jnp.where(mask, logw, -jnp.inf)
            sums.append(jnp.sum(ls, axis=-1, keepdims=True))
            logws.append(logw)
        return sums, logws

    def weighted_values(logws, runs, vts):
        return [_dot(jnp.exp2(logw + run).astype(BF16), vt) for logw, run, vt in zip(logws, runs, vts)]

    def q_loop(i, carry):
        chunk = step * nq + i
        slot = i % 2

        @pl.when(i == 0)
        def _():
            for dma in cast_loads(chunk, slot):
                dma.start()

        @pl.when(i + 1 < nq)
        def _():
            for dma in cast_loads(chunk + 1, 1 - slot):
                dma.start()

        q0 = pl.multiple_of(i * t, t)
        qts = [q_ref[pl.ds(q0, t), sl] for sl in heads]

        def kv(k0):
            return [k_ref[pl.ds(k0, t), sl] for sl in heads], [v_ref[pl.ds(k0, t), sl] for sl in heads]

        kts, vts = kv(q0)
        runs, logws = scores(qts, kts, later2, causal)
        _, logws_meta = scores(qts, [km_ref[:, sl] for sl in heads], later_m2, meta_ok)
        accs = weighted_values(logws, [0.0] * ATTN_HEADS, vts)

        def kv_loop(jj, c):
            runs, accs = c
            kts, vts = kv(pl.multiple_of((i - 1 - jj) * t, t))
            sums, logws = scores(qts, kts, later2, None)
            pvs = weighted_values(logws, runs, vts)
            return [run + s for run, s in zip(runs, sums)], [acc + pv for acc, pv in zip(accs, pvs)]

        runs, accs = lax.fori_loop(0, i, kv_loop, (runs, accs))
        pvs = weighted_values(logws_meta, runs, [vm_ref[:, sl] for sl in heads])
        for sl, acc, pv in zip(heads, accs, pvs):
            o_ref[pl.ds(q0, t), sl] = (acc + pv).astype(o_ref.dtype)

        for dma in cast_loads(chunk, slot):
            dma.wait()

        @pl.when(i >= 2)
        def _():
            for dma in cast_stores(chunk - 2, slot):
                dma.wait()

        for a in range(n_cast):
            cast_out[a][slot] = cast_in[a][slot].astype(BF16)
        for dma in cast_stores(chunk, slot):
            dma.start()
        return carry

    lax.fori_loop(0, nq, q_loop, 0)
    for back in (2, 1):
        for dma in cast_stores(step * nq + nq - back, (nq - back) % 2):
            dma.wait()


def attention(qkv, kv_meta, cast_srcs, *, batch, seq):
    w = ATTN_HEADS * HEAD_DIM
    nb = D_ATTN // w
    n_cast = len(cast_srcs)
    n_chunks = batch * nb * (seq // ATTN_TILE)
    chunk_shapes = [(a.shape[0] // n_chunks, a.shape[1]) for a in cast_srcs]
    any_spec = pl.BlockSpec(memory_space=pl.ANY)
    outs = pl.pallas_call(
        functools.partial(_attn_kernel, seq=seq, n_cast=n_cast),
        grid=(batch, nb),
        in_specs=[
            pl.BlockSpec((seq, w), lambda b, hp: (b, 2 * nb + hp)),
            pl.BlockSpec((seq, w), lambda b, hp: (b, hp)),
            pl.BlockSpec((seq, w), lambda b, hp: (b, nb + hp)),
            pl.BlockSpec((META_KEYS, w), lambda b, hp: (0, hp)),
            pl.BlockSpec((META_KEYS, w), lambda b, hp: (0, nb + hp)),
        ] + [any_spec] * n_cast,
        out_specs=[pl.BlockSpec((seq, w), lambda b, hp: (b, hp))] + [any_spec] * n_cast,
        out_shape=[jax.ShapeDtypeStruct((batch * seq, D_ATTN), BF16)]
        + [jax.ShapeDtypeStruct(a.shape, BF16) for a in cast_srcs],
        scratch_shapes=[pltpu.VMEM((2,) + cs, F32) for cs in chunk_shapes]
        + [pltpu.VMEM((2,) + cs, BF16) for cs in chunk_shapes]
        + [pltpu.SemaphoreType.DMA((2, n_cast)), pltpu.SemaphoreType.DMA((2, n_cast))],
        compiler_params=_params("parallel", "parallel"),
        name="attention",
    )(qkv, qkv, qkv, kv_meta, kv_meta, *cast_srcs)
    return outs[0], outs[1:]


def _oproj_router_kernel(o_ref, wo_ref, h_ref, g_ref, wr_ref, h3_ref, hn_ref, idx_ref, gate_ref):
    h3 = h_ref[...] + _dot(o_ref[...], wo_ref[...])
    h3_ref[...] = h3
    hn = _rms(h3, g_ref[...])
    hn_ref[...] = hn
    hn_hi, hn_lo = _split_bf16(hn)
    wr_hi, wr_lo = _split_bf16(wr_ref[...])
    logits = _dot(hn_hi, wr_hi) + (_dot(hn_hi, wr_lo) + _dot(hn_lo, wr_hi))
    e = lax.broadcasted_iota(jnp.int32, logits.shape, 1)
    m1 = jnp.max(logits, axis=-1, keepdims=True)
    i1 = jnp.min(jnp.where(logits == m1, e, N_EXPERTS), axis=-1, keepdims=True)
    rest = jnp.where(e == i1, -jnp.inf, logits)
    m2 = jnp.max(rest, axis=-1, keepdims=True)
    i2 = jnp.min(jnp.where(rest == m2, e, N_EXPERTS), axis=-1, keepdims=True)
    p = jnp.exp(m2 - m1)
    g1 = 1.0 / (1.0 + p)
    k = lax.broadcasted_iota(jnp.int32, idx_ref.shape, 1)
    idx_ref[...] = jnp.where(k == 0, i1, i2)
    gate_ref[...] = jnp.where(k == 0, g1, p * g1)


def oproj_router(o, wo, h, g, w_router, *, tm):
    m, d = h.shape
    return pl.pallas_call(
        _oproj_router_kernel,
        grid=(m // tm,),
        in_specs=[
            pl.BlockSpec((tm, d), lambda i: (i, 0)),
            _resident((d, d), lambda i: (0, 0)),
            pl.BlockSpec((tm, d), lambda i: (i, 0)),
            _resident((1, d), lambda i: (0, 0)),
            _resident((d, N_EXPERTS), lambda i: (0, 0)),
        ],
        out_specs=[
            pl.BlockSpec((tm, d), lambda i: (i, 0)),
            pl.BlockSpec((tm, d), lambda i: (i, 0)),
            pl.BlockSpec((tm, TOP_K), lambda i: (i, 0)),
            pl.BlockSpec((tm, TOP_K), lambda i: (i, 0)),
        ],
        out_shape=[
            jax.ShapeDtypeStruct((m, d), F32),
            jax.ShapeDtypeStruct((m, d), F32),
            jax.ShapeDtypeStruct((m, TOP_K), jnp.int32),
            jax.ShapeDtypeStruct((m, TOP_K), F32),
        ],
        compiler_params=_params("parallel"),
        name="oproj_router",
    )(o, wo, h, g, w_router)


ROW_DMA_UNROLL = 8


def _dispatch_kernel(dest_ref, gend_ref, nu_ref, hn_ref, x_hbm, zero_ref, sem, zsem, *, tm, tile, n_tiles):
    base = pl.program_id(0) * (tm * TOP_K)

    @pl.when(pl.program_id(0) == 0)
    def _():
        zero_ref[...] = jnp.zeros_like(zero_ref)

        def zero_tile(start):
            return pltpu.make_async_copy(zero_ref, x_hbm.at[pl.ds(pl.multiple_of(start, tile), tile), :], zsem)

        def nonempty(e):
            return gend_ref[e] > (gend_ref[e - 1] if e else 0)

        for e in range(N_EXPERTS):
            @pl.when(nonempty(e))
            def _():
                zero_tile(gend_ref[e] - tile).start()

        for e in range(N_EXPERTS):
            @pl.when(nonempty(e))
            def _():
                zero_tile(0).wait()

        def tail_start(j, c):
            zero_tile(j * tile).start()
            return c

        def tail_wait(j, c):
            zero_tile(0).wait()
            return c

        lax.fori_loop(nu_ref[0], n_tiles, tail_start, 0)
        lax.fori_loop(nu_ref[0], n_tiles, tail_wait, 0)

    def row(r, k, dst):
        return pltpu.make_async_copy(hn_ref.at[pl.ds(r, 1), :], x_hbm.at[pl.ds(dst, 1), :], sem)

    def issue(r, c):
        for k in range(TOP_K):
            row(r, k, dest_ref[base + TOP_K * r + k]).start()
        return c

    lax.fori_loop(0, tm, issue, 0, unroll=ROW_DMA_UNROLL)

    def wait(r, c):
        for k in range(TOP_K):
            row(r, k, 0).wait()
        return c

    lax.fori_loop(0, tm, wait, 0, unroll=ROW_DMA_UNROLL)


def moe_dispatch(hn, dest, gend, n_used, padded_rows, *, tm, tile):
    t, d = hn.shape
    grid_spec = pltpu.PrefetchScalarGridSpec(
        num_scalar_prefetch=3,
        grid=(t // tm,),
        in_specs=[pl.BlockSpec((tm, d), lambda i, *_: (i, 0))],
        out_specs=pl.BlockSpec(memory_space=pl.ANY),
        scratch_shapes=[pltpu.VMEM((tile, d), F32), pltpu.SemaphoreType.DMA(()), pltpu.SemaphoreType.DMA(())],
    )
    return pl.pallas_call(
        functools.partial(_dispatch_kernel, tm=tm, tile=tile, n_tiles=padded_rows // tile),
        grid_spec=grid_spec,
        out_shape=jax.ShapeDtypeStruct((padded_rows, d), F32),
        compiler_params=_params("arbitrary"),
        name="moe_dispatch",
    )(dest.reshape(-1), gend, n_used, hn)


def _moe_kernel(te_ref, nu_ref, x_ref, wg_ref, wu_ref, wd_ref, o_ref, xb_ref):
    i = pl.program_id(0)
    f = pl.program_id(1)
    used = i < nu_ref[0]

    @pl.when(f == 0)
    def _():
        xb_ref[...] = x_ref[...].astype(BF16)
        o_ref[...] = jnp.zeros_like(o_ref)

    @pl.when(used)
    def _():
        o_ref[...] += _swiglu_partial(xb_ref[...], wg_ref, wu_ref, wd_ref)


def moe_experts(x, tile_e, n_used, w_gate, w_up, w_down, *, tm, tf):
    p, d = x.shape
    nf = w_gate.shape[2] // tf

    def wmap_in(i, f, te, nu):
        return (te[i], 0, jnp.where(i < nu[0], f, nf - 1))

    def wmap_down(i, f, te, nu):
        return (te[i], jnp.where(i < nu[0], f, nf - 1), 0)

    grid_spec = pltpu.PrefetchScalarGridSpec(
        num_scalar_prefetch=2,
        grid=(p // tm, nf),
        in_specs=[
            pl.BlockSpec((tm, d), lambda i, f, te, nu: (i, 0)),
            pl.BlockSpec((None, d, tf), wmap_in),
            pl.BlockSpec((None, d, tf), wmap_in),
            pl.BlockSpec((None, tf, d), wmap_down),
        ],
        out_specs=pl.BlockSpec((tm, d), lambda i, f, te, nu: (i, 0)),
        scratch_shapes=[pltpu.VMEM((tm, d), BF16)],
    )
    return pl.pallas_call(
        _moe_kernel,
        grid_spec=grid_spec,
        out_shape=jax.ShapeDtypeStruct((p, d), F32),
        compiler_params=_params("arbitrary", "arbitrary"),
        name="moe_experts",
    )(tile_e, n_used, x, w_gate, w_up, w_down)


def _combine_kernel(dest_ref, h_ref, gate_ref, g_ref, y_hbm, o_ref, ybuf, sem, *, tm, n_steps):
    i = pl.program_id(0)

    def row(slot, r, k, src):
        return pltpu.make_async_copy(y_hbm.at[pl.ds(src, 1), :], ybuf.at[slot, k, pl.ds(r, 1), :], sem.at[slot])

    def fetch(step, slot):
        base = step * (tm * TOP_K)

        def issue(r, c):
            for k in range(TOP_K):
                row(slot, r, k, dest_ref[base + TOP_K * r + k]).start()
            return c

        lax.fori_loop(0, tm, issue, 0, unroll=ROW_DMA_UNROLL)

    @pl.when(i == 0)
    def _():
        fetch(0, 0)

    @pl.when(i + 1 < n_steps)
    def _():
        fetch(i + 1, (i + 1) % 2)

    slot = i % 2

    def wait(r, c):
        for k in range(TOP_K):
            row(slot, r, k, 0).wait()
        return c

    lax.fori_loop(0, tm, wait, 0, unroll=ROW_DMA_UNROLL)
    gate = gate_ref[...]
    y = h_ref[...] + gate[:, 0:1] * ybuf[slot, 0] + gate[:, 1:2] * ybuf[slot, 1]
    o_ref[...] = _rms(y, g_ref[...])


def moe_combine_norm(h, gates, y, dest, g, *, tm):
    t, d = h.shape
    n_steps = t // tm
    grid_spec = pltpu.PrefetchScalarGridSpec(
        num_scalar_prefetch=1,
        grid=(n_steps,),
        in_specs=[
            pl.BlockSpec((tm, d), lambda i, dest_ref: (i, 0)),
            pl.BlockSpec((tm, TOP_K), lambda i, dest_ref: (i, 0)),
            pl.BlockSpec((1, d), lambda i, dest_ref: (0, 0)),
            pl.BlockSpec(memory_space=pl.ANY),
        ],
        out_specs=pl.BlockSpec((tm, d), lambda i, dest_ref: (i, 0)),
        scratch_shapes=[pltpu.VMEM((2, TOP_K, tm, d), F32), pltpu.SemaphoreType.DMA((2,))],
    )
    return pl.pallas_call(
        functools.partial(_combine_kernel, tm=tm, n_steps=n_steps),
        grid_spec=grid_spec,
        out_shape=jax.ShapeDtypeStruct((t, d), F32),
        compiler_params=_params("arbitrary"),
        name="moe_combine_norm",
    )(dest.reshape(-1), h, gates, g, y)


def routing_tables(idx, *, tm):
    t = idx.shape[0]
    rows = TOP_K * t
    n_tiles = rows // tm + N_EXPERTS
    e_flat = idx.reshape(rows)
    onehot = (e_flat[:, None] == jnp.arange(N_EXPERTS, dtype=jnp.int32)[None, :]).astype(jnp.int32)
    csum = jnp.cumsum(onehot, axis=0)
    rank = jnp.sum((csum - onehot) * onehot, axis=1)
    counts = csum[-1]
    padded = ((counts + tm - 1) // tm) * tm
    gend = jnp.cumsum(padded)
    gstart = gend - padded
    dest = (jnp.sum(onehot * gstart[None, :], axis=1) + rank).astype(jnp.int32).reshape(t, TOP_K)
    n_used = (gend[-1] // tm).astype(jnp.int32).reshape(1)
    tile_start = jnp.minimum(jnp.arange(n_tiles, dtype=jnp.int32), n_used - 1) * tm
    tile_e = jnp.sum((tile_start[:, None] >= gend[None, :]).astype(jnp.int32), axis=1).astype(jnp.int32)
    return dest, tile_e, n_used, gend.astype(jnp.int32), n_tiles * tm


def kernel(x, meta_tokens, mix_norm_g, ffn_norm_g, conv_pw1_w, conv_pw1_b, conv_dw_w, conv_dw_b,
           conv_ln_g, conv_ln_b, conv_pw2_w, conv_pw2_b, kv_norm_g, w_kv, w_q, w_o, ffn_w_gate,
           ffn_w_up, ffn_w_down, moe_router, moe_w_gate, moe_w_up, moe_w_down, final_norm_g):
    batch, seq, d = x.shape
    t = batch * seq
    row = lambda v: v.reshape(1, -1).astype(F32)

    pw1 = conv_pw1_w[0].astype(BF16)
    pw1_b = row(conv_pw1_b[0])
    dw_w = jnp.pad(conv_dw_w[0], ((0, CONV_HALO - CONV_WIDTH), (0, 0)))
    pw2 = conv_pw2_w[0].astype(BF16)
    w_kvq = jnp.concatenate([w_kv.astype(BF16), w_q[0].astype(BF16)], axis=1)
    g_kvq = jnp.stack([kv_norm_g, mix_norm_g[1]]).astype(F32)
    wo = w_o[0].astype(BF16)
    fg, fu, fd = ffn_w_gate[0].astype(BF16), ffn_w_up[0].astype(BF16), ffn_w_down[0].astype(BF16)
    moe_w = [moe_w_gate[0], moe_w_up[0], moe_w_down[0]]

    def layer0(h, halo0, *, b, l, tm, tl):
        u = rms_glu(h, row(mix_norm_g[0]), pw1, pw1_b, tm=tm, tn=512)
        h1 = conv_block(u.reshape(b, l, d), halo0, h.reshape(b, l, d), dw_w, row(conv_dw_b[0]),
                        row(conv_ln_g[0]), row(conv_ln_b[0]), pw2, row(conv_pw2_b[0]), tl=tl)
        h2 = ffn_block(h1.reshape(b * l, d), row(ffn_norm_g[0]), fg, fu, fd, tm=tm, tf=512)
        return u, h2

    hm = jnp.pad(meta_tokens.astype(F32), ((0, META_PAD - N_META), (0, 0)))
    um, hm2 = layer0(hm, jnp.zeros((CONV_HALO, d), BF16), b=1, l=META_PAD, tm=META_PAD, tl=META_PAD)
    kv_meta = rms_proj2(hm2, g_kvq, w_kvq, n_cols=2 * D_ATTN, n_first_cols=2 * D_ATTN, second_scale=1.0,
                        tm=META_PAD, tn=512)
    kv_meta = jnp.pad(kv_meta[:N_META], ((0, META_KEYS - N_META), (0, 0)))

    halo0 = jnp.concatenate([jnp.zeros((CONV_HALO - N_META, d), BF16), um[:N_META]], axis=0)
    _, h2 = layer0(x.reshape(t, d), halo0, b=batch, l=seq, tm=512, tl=512)

    qkv = rms_proj2(h2, g_kvq, w_kvq, n_cols=3 * D_ATTN, n_first_cols=2 * D_ATTN, second_scale=Q_PRESCALE,
                    tm=1024, tn=512)
    o, moe_bf16 = attention(qkv, kv_meta, [w.reshape(-1, w.shape[-1]) for w in moe_w], batch=batch, seq=seq)
    mg, mu, md = [wb.reshape(w.shape) for wb, w in zip(moe_bf16, moe_w)]
    h3, hn3, idx, gates = oproj_router(o, wo, h2, row(ffn_norm_g[1]), moe_router[0].astype(F32), tm=512)

    moe_tm = 512
    dest, tile_e, n_used, gend, padded_rows = routing_tables(idx, tm=moe_tm)
    xs = moe_dispatch(hn3, dest, gend, n_used, padded_rows, tm=512, tile=moe_tm)
    ys = moe_experts(xs, tile_e, n_used, mg, mu, md, tm=moe_tm, tf=512)
    out = moe_combine_norm(h3, gates, ys, dest, row(final_norm_g), tm=256)
    return out.reshape(batch, seq, d)
```

```python
import functools
import math

import jax
import jax.numpy as jnp
from jax import lax
from jax.experimental import pallas as pl
from jax.experimental.pallas import tpu as pltpu

D_MODEL = 2048
N_META = 16
META_PAD = 32
N_HEADS = 16
HEAD_DIM = 128
D_ATTN = N_HEADS * HEAD_DIM
CONV_WIDTH = 31
CONV_HALO = 32
D_FF = 5632
N_EXPERTS = 8
TOP_K = 2
RMS_EPS = 1e-6
LN_EPS = 1e-5

VMEM_LIMIT_BYTES = 56 * 1024 * 1024
F32 = jnp.float32
BF16 = jnp.bfloat16
LOG2_E = math.log2(math.e)


def _dot(a, b):
    return jnp.dot(a, b, preferred_element_type=F32)


def _params(*sem):
    return pltpu.CompilerParams(dimension_semantics=sem, vmem_limit_bytes=VMEM_LIMIT_BYTES)


def _resident(shape, index_map):
    return pl.BlockSpec(shape, index_map, pipeline_mode=pl.Buffered(1))


def _rms(h, g):
    var = jnp.mean(h * h, axis=-1, keepdims=True)
    return h * lax.rsqrt(var + RMS_EPS) * g


def _sigmoid(x):
    return 1.0 / (1.0 + jnp.exp(-x))


def _split_bf16(x):
    hi = x.astype(BF16)
    lo = (x - hi.astype(F32)).astype(BF16)
    return hi, lo


def _rms_glu_kernel(h_ref, g_ref, wa_ref, wg_ref, ba_ref, bg_ref, o_ref, hn_ref):
    @pl.when(pl.program_id(1) == 0)
    def _():
        hn_ref[...] = _rms(h_ref[...], g_ref[...]).astype(BF16)

    hn = hn_ref[...]
    half = o_ref.shape[1] // 2
    for sl in (slice(0, half), slice(half, 2 * half)):
        a = _dot(hn, wa_ref[:, sl]) + ba_ref[:, sl]
        g = _dot(hn, wg_ref[:, sl]) + bg_ref[:, sl]
        o_ref[:, sl] = (a * _sigmoid(g)).astype(o_ref.dtype)


def rms_glu(h, g, w, b, *, tm, tn):
    m, d = h.shape
    nb = d // tn
    return pl.pallas_call(
        _rms_glu_kernel,
        grid=(m // tm, nb),
        in_specs=[
            pl.BlockSpec((tm, d), lambda i, n: (i, 0)),
            pl.BlockSpec((1, d), lambda i, n: (0, 0)),
            pl.BlockSpec((d, tn), lambda i, n: (0, n)),
            pl.BlockSpec((d, tn), lambda i, n: (0, n + nb)),
            pl.BlockSpec((1, tn), lambda i, n: (0, n)),
            pl.BlockSpec((1, tn), lambda i, n: (0, n + nb)),
        ],
        out_specs=pl.BlockSpec((tm, tn), lambda i, n: (i, n)),
        out_shape=jax.ShapeDtypeStruct((m, d), BF16),
        scratch_shapes=[pltpu.VMEM((tm, d), BF16)],
        compiler_params=_params("parallel", "arbitrary"),
        name="rms_glu",
    )(h, g, w, w, b, b)


def _rms_proj2_kernel(h_ref, g_ref, w_ref, o_ref, hn_ref, *, n_first, second_scale):
    n = pl.program_id(1)

    @pl.when(n == 0)
    def _():
        h = h_ref[...]
        y = h * lax.rsqrt(jnp.mean(h * h, axis=-1, keepdims=True) + RMS_EPS)
        hn_ref[0] = (y * g_ref[0:1, :]).astype(BF16)
        hn_ref[1] = (y * g_ref[1:2, :]).astype(BF16)

    second = n >= n_first
    r = _dot(hn_ref[second.astype(jnp.int32)], w_ref[...])
    o_ref[...] = (r * jnp.where(second, second_scale, 1.0)).astype(o_ref.dtype)


def rms_proj2(h, g2, w, *, n_cols, n_first_cols, second_scale, tm, tn):
    m, d = h.shape
    return pl.pallas_call(
        functools.partial(_rms_proj2_kernel, n_first=n_first_cols // tn, second_scale=second_scale),
        grid=(m // tm, n_cols // tn),
        in_specs=[
            pl.BlockSpec((tm, d), lambda i, j: (i, 0)),
            pl.BlockSpec((2, d), lambda i, j: (0, 0)),
            pl.BlockSpec((d, tn), lambda i, j: (0, j)),
        ],
        out_specs=pl.BlockSpec((tm, tn), lambda i, j: (i, j)),
        out_shape=jax.ShapeDtypeStruct((m, n_cols), BF16),
        scratch_shapes=[pltpu.VMEM((2, tm, d), BF16)],
        compiler_params=_params("parallel", "arbitrary"),
        name="rms_proj2",
    )(h, g2, w)


CONV_LANES = 256
CONV_ROWS = 64


def _conv_kernel(u_ref, uh_ref, halo0_ref, h_ref, dw_ref, dwb_ref, lng_ref, lnb_ref,
                 w2_ref, b2_ref, o_ref, xp_ref, sh_ref, cv_ref, *, tl):
    i = pl.program_id(1)
    d = u_ref.shape[-1]

    @pl.when(i == 0)
    def _():
        xp_ref[0:CONV_HALO, :] = halo0_ref[...].astype(F32)

    @pl.when(i > 0)
    def _():
        xp_ref[0:CONV_HALO, :] = uh_ref[...].astype(F32)

    xp_ref[CONV_HALO:CONV_HALO + tl, :] = u_ref[...].astype(F32)

    n_sh = tl + CONV_HALO - 8
    rows = min(CONV_ROWS, tl)

    def lane_block(cb, carry):
        c0 = pl.multiple_of(cb * CONV_LANES, CONV_LANES)
        for r in range(1, 8):
            sh_ref[r, 0:n_sh, :] = xp_ref[pl.ds(r, n_sh), pl.ds(c0, CONV_LANES)]

        def row_chunk(rc, carry2):
            r0 = pl.multiple_of(rc * rows, rows)
            acc = jnp.zeros((rows, CONV_LANES), F32)
            for w in range(CONV_WIDTH):
                a, r = divmod(CONV_HALO - (CONV_WIDTH - 1) + w, 8)
                if r == 0:
                    xs = xp_ref[pl.ds(r0 + 8 * a, rows), pl.ds(c0, CONV_LANES)]
                else:
                    xs = sh_ref[r, pl.ds(r0 + 8 * a, rows), :]
                acc = acc + xs * dw_ref[pl.ds(w, 1), pl.ds(c0, CONV_LANES)]
            cv_ref[pl.ds(r0, rows), pl.ds(c0, CONV_LANES)] = acc + dwb_ref[:, pl.ds(c0, CONV_LANES)]
            return carry2

        lax.fori_loop(0, tl // rows, row_chunk, 0)
        return carry

    lax.fori_loop(0, d // CONV_LANES, lane_block, 0)

    c = cv_ref[...]
    mu = jnp.mean(c, axis=-1, keepdims=True)
    cc = c - mu
    var = jnp.mean(cc * cc, axis=-1, keepdims=True)
    y = cc * lax.rsqrt(var + LN_EPS) * lng_ref[...] + lnb_ref[...]
    s = (y * _sigmoid(y)).astype(BF16)
    o_ref[...] = h_ref[...] + _dot(s, w2_ref[...]) + b2_ref[...]


def conv_block(u, halo0, h, dw_w, dw_b, ln_g, ln_b, w2, b2, *, tl):
    b, l, d = u.shape
    hb = tl // CONV_HALO
    return pl.pallas_call(
        functools.partial(_conv_kernel, tl=tl),
        grid=(b, l // tl),
        in_specs=[
            pl.BlockSpec((None, tl, d), lambda bi, i: (bi, i, 0)),
            pl.BlockSpec((None, CONV_HALO, d), lambda bi, i: (bi, jnp.maximum(i * hb - 1, 0), 0)),
            _resident((CONV_HALO, d), lambda bi, i: (0, 0)),
            pl.BlockSpec((None, tl, d), lambda bi, i: (bi, i, 0)),
            _resident((CONV_HALO, d), lambda bi, i: (0, 0)),
            _resident((1, d), lambda bi, i: (0, 0)),
            _resident((1, d), lambda bi, i: (0, 0)),
            _resident((1, d), lambda bi, i: (0, 0)),
            _resident((d, d), lambda bi, i: (0, 0)),
            _resident((1, d), lambda bi, i: (0, 0)),
        ],
        out_specs=pl.BlockSpec((None, tl, d), lambda bi, i: (bi, i, 0)),
        out_shape=jax.ShapeDtypeStruct((b, l, d), F32),
        scratch_shapes=[
            pltpu.VMEM((tl + CONV_HALO, d), F32),
            pltpu.VMEM((8, tl + CONV_HALO, CONV_LANES), F32),
            pltpu.VMEM((tl, d), F32),
        ],
        compiler_params=_params("parallel", "arbitrary"),
        name="conv_block",
    )(u, u, halo0, h, dw_w, dw_b, ln_g, ln_b, w2, b2)


def _swiglu_partial(x, wg_ref, wu_ref, wd_ref):
    half = wg_ref.shape[1] // 2
    acts = []
    for sl in (slice(0, half), slice(half, 2 * half)):
        a = _dot(x, wg_ref[:, sl])
        u = _dot(x, wu_ref[:, sl])
        acts.append((a * _sigmoid(a) * u).astype(BF16))
    return _dot(jnp.concatenate(acts, axis=1), wd_ref[...])


def _ffn_kernel(h_ref, g_ref, wg_ref, wu_ref, wd_ref, o_ref, hn_ref):
    @pl.when(pl.program_id(1) == 0)
    def _():
        h = h_ref[...]
        hn_ref[...] = _rms(h, g_ref[...]).astype(BF16)
        o_ref[...] = h

    o_ref[...] += _swiglu_partial(hn_ref[...], wg_ref, wu_ref, wd_ref)


def ffn_block(h, g, w_gate, w_up, w_down, *, tm, tf):
    m, d = h.shape
    ff = w_gate.shape[1]
    return pl.pallas_call(
        _ffn_kernel,
        grid=(m // tm, ff // tf),
        in_specs=[
            pl.BlockSpec((tm, d), lambda i, f: (i, 0)),
            pl.BlockSpec((1, d), lambda i, f: (0, 0)),
            pl.BlockSpec((d, tf), lambda i, f: (0, f)),
            pl.BlockSpec((d, tf), lambda i, f: (0, f)),
            pl.BlockSpec((tf, d), lambda i, f: (f, 0)),
        ],
        out_specs=pl.BlockSpec((tm, d), lambda i, f: (i, 0)),
        out_shape=jax.ShapeDtypeStruct((m, d), F32),
        scratch_shapes=[pltpu.VMEM((tm, d), BF16)],
        compiler_params=_params("parallel", "arbitrary"),
        name="ffn_block",
    )(h, g, w_gate, w_up, w_down)


ATTN_TILE = 256
ATTN_HEADS = 4
META_KEYS = 128
Q_PRESCALE = -(HEAD_DIM ** -0.5) * LOG2_E


def _attn_kernel(q_ref, k_ref, v_ref, km_ref, vm_ref, *rest, seq, n_cast):
    cast_src = rest[:n_cast]
    o_ref = rest[n_cast]
    cast_dst = rest[n_cast + 1:2 * n_cast + 1]
    cast_in = rest[2 * n_cast + 1:3 * n_cast + 1]
    cast_out = rest[3 * n_cast + 1:4 * n_cast + 1]
    sem_in, sem_out = rest[4 * n_cast + 1:]
    t = ATTN_TILE
    nq = seq // t
    step = pl.program_id(0) * pl.num_programs(1) + pl.program_id(1)

    def chunk_rows(a, chunk):
        rows = cast_in[a].shape[1]
        return pl.ds(pl.multiple_of(chunk * rows, 16), rows)

    def cast_loads(chunk, slot):
        return [pltpu.make_async_copy(cast_src[a].at[chunk_rows(a, chunk), :], cast_in[a].at[slot], sem_in.at[slot, a])
                for a in range(n_cast)]

    def cast_stores(chunk, slot):
        return [pltpu.make_async_copy(cast_out[a].at[slot], cast_dst[a].at[chunk_rows(a, chunk), :], sem_out.at[slot, a])
                for a in range(n_cast)]

    row = lax.broadcasted_iota(jnp.int32, (t, t), 0)
    col = lax.broadcasted_iota(jnp.int32, (t, t), 1)
    later = (row > col).astype(BF16)
    later_m = later[:META_KEYS, :META_KEYS]
    causal = col < row
    meta_ok = lax.broadcasted_iota(jnp.int32, (t, META_KEYS), 1) < N_META

    heads = [slice(hh * HEAD_DIM, (hh + 1) * HEAD_DIM) for hh in range(ATTN_HEADS)]

    def scores(qts, kts, later_mat, mask):
        ys = [lax.dot_general(qt, kt, (((1,), (1,)), ((), ())), preferred_element_type=F32)
              for qt, kt in zip(qts, kts)]
        sums, parts, own = [], [], []
        for y in ys:
            e = jnp.exp2(jnp.minimum(y, -y))
            ls = jnp.minimum(y, 0.0) - jnp.log2(1.0 + e)
            if mask is not None:
                ls = jnp.where(mask, ls, 0.0)
            sums.append(jnp.sum(ls, axis=-1, keepdims=True))
            parts.append(ls.astype(BF16))
            own.append(ls - y)
        logws = []
        for part, o in zip(parts, own):
            logw = o + _dot(part, later_mat)
            if mask is not None:
                logw = jnp.where(mask, logw, -jnp.inf)
            logws.append(logw)
        return sums, logws

    def weighted_values(logws, runs, vts):
        return [_dot(jnp.exp2(logw + run).astype(BF16), vt) for logw, run, vt in zip(logws, runs, vts)]

    def q_loop(i, carry):
        chunk = step * nq + i
        slot = i % 2

        @pl.when(i == 0)
        def _():
            for dma in cast_loads(chunk, slot):
                dma.start()

        @pl.when(i + 1 < nq)
        def _():
            for dma in cast_loads(chunk + 1, 1 - slot):
                dma.start()

        q0 = pl.multiple_of(i * t, t)
        qts = [q_ref[pl.ds(q0, t), sl] for sl in heads]

        def kv(k0):
            return [k_ref[pl.ds(k0, t), sl] for sl in heads], [v_ref[pl.ds(k0, t), sl] for sl in heads]

        kts, vts = kv(q0)
        runs, logws = scores(qts, kts, later, causal)
        _, logws_meta = scores(qts, [km_ref[:, sl] for sl in heads], later_m, meta_ok)
        accs = weighted_values(logws, [0.0] * ATTN_HEADS, vts)

        def kv_loop(jj, c):
            runs, accs = c
            kts, vts = kv(pl.multiple_of((i - 1 - jj) * t, t))
            sums, logws = scores(qts, kts, later, None)
            pvs = weighted_values(logws, runs, vts)
            return [run + s for run, s in zip(runs, sums)], [acc + pv for acc, pv in zip(accs, pvs)]

        runs, accs = lax.fori_loop(0, i, kv_loop, (runs, accs))
        pvs = weighted_values(logws_meta, runs, [vm_ref[:, sl] for sl in heads])
        for sl, acc, pv in zip(heads, accs, pvs):
            o_ref[pl.ds(q0, t), sl] = (acc + pv).astype(o_ref.dtype)

        for dma in cast_loads(chunk, slot):
            dma.wait()

        @pl.when(i >= 2)
        def _():
            for dma in cast_stores(chunk - 2, slot):
                dma.wait()

        for a in range(n_cast):
            cast_out[a][slot] = cast_in[a][slot].astype(BF16)
        for dma in cast_stores(chunk, slot):
            dma.start()
        return carry

    lax.fori_loop(0, nq, q_loop, 0)
    for back in (2, 1):
        for dma in cast_stores(step * nq + nq - back, (nq - back) % 2):
            dma.wait()


def attention(qkv, kv_meta, cast_srcs, *, batch, seq):
    w = ATTN_HEADS * HEAD_DIM
    nb = D_ATTN // w
    n_cast = len(cast_srcs)
    n_chunks = batch * nb * (seq // ATTN_TILE)
    chunk_shapes = [(a.shape[0] // n_chunks, a.shape[1]) for a in cast_srcs]
    any_spec = pl.BlockSpec(memory_space=pl.ANY)
    outs = pl.pallas_call(
        functools.partial(_attn_kernel, seq=seq, n_cast=n_cast),
        grid=(batch, nb),
        in_specs=[
            pl.BlockSpec((seq, w), lambda b, hp: (b, 2 * nb + hp)),
            pl.BlockSpec((seq, w), lambda b, hp: (b, hp)),
            pl.BlockSpec((seq, w), lambda b, hp: (b, nb + hp)),
            pl.BlockSpec((META_KEYS, w), lambda b, hp: (0, hp)),
            pl.BlockSpec((META_KEYS, w), lambda b, hp: (0, nb + hp)),
        ] + [any_spec] * n_cast,
        out_specs=[pl.BlockSpec((seq, w), lambda b, hp: (b, hp))] + [any_spec] * n_cast,
        out_shape=[jax.ShapeDtypeStruct((batch * seq, D_ATTN), BF16)]
        + [jax.ShapeDtypeStruct(a.shape, BF16) for a in cast_srcs],
        scratch_shapes=[pltpu.VMEM((2,) + cs, F32) for cs in chunk_shapes]
        + [pltpu.VMEM((2,) + cs, BF16) for cs in chunk_shapes]
        + [pltpu.SemaphoreType.DMA((2, n_cast)), pltpu.SemaphoreType.DMA((2, n_cast))],
        compiler_params=_params("parallel", "parallel"),
        name="attention",
    )(qkv, qkv, qkv, kv_meta, kv_meta, *cast_srcs)
    return outs[0], outs[1:]


def _oproj_router_kernel(o_ref, wo_ref, h_ref, g_ref, wr_ref, h3_ref, hn_ref, idx_ref, gate_ref):
    h3 = h_ref[...] + _dot(o_ref[...], wo_ref[...])
    h3_ref[...] = h3
    hn = _rms(h3, g_ref[...])
    hn_ref[...] = hn
    hn_hi, hn_lo = _split_bf16(hn)
    wr_hi, wr_lo = _split_bf16(wr_ref[...])
    logits = _dot(hn_hi, wr_hi) + (_dot(hn_hi, wr_lo) + _dot(hn_lo, wr_hi))
    e = lax.broadcasted_iota(jnp.int32, logits.shape, 1)
    m1 = jnp.max(logits, axis=-1, keepdims=True)
    i1 = jnp.min(jnp.where(logits == m1, e, N_EXPERTS), axis=-1, keepdims=True)
    rest = jnp.where(e == i1, -jnp.inf, logits)
    m2 = jnp.max(rest, axis=-1, keepdims=True)
    i2 = jnp.min(jnp.where(rest == m2, e, N_EXPERTS), axis=-1, keepdims=True)
    p = jnp.exp(m2 - m1)
    g1 = 1.0 / (1.0 + p)
    k = lax.broadcasted_iota(jnp.int32, idx_ref.shape, 1)
    idx_ref[...] = jnp.where(k == 0, i1, i2)
    gate_ref[...] = jnp.where(k == 0, g1, p * g1)


def oproj_router(o, wo, h, g, w_router, *, tm):
    m, d = h.shape
    return pl.pallas_call(
        _oproj_router_kernel,
        grid=(m // tm,),
        in_specs=[
            pl.BlockSpec((tm, d), lambda i: (i, 0)),
            _resident((d, d), lambda i: (0, 0)),
            pl.BlockSpec((tm, d), lambda i: (i, 0)),
            _resident((1, d), lambda i: (0, 0)),
            _resident((d, N_EXPERTS), lambda i: (0, 0)),
        ],
        out_specs=[
            pl.BlockSpec((tm, d), lambda i: (i, 0)),
            pl.BlockSpec((tm, d), lambda i: (i, 0)),
            pl.BlockSpec((tm, TOP_K), lambda i: (i, 0)),
            pl.BlockSpec((tm, TOP_K), lambda i: (i, 0)),
        ],
        out_shape=[
            jax.ShapeDtypeStruct((m, d), F32),
            jax.ShapeDtypeStruct((m, d), F32),
            jax.ShapeDtypeStruct((m, TOP_K), jnp.int32),
            jax.ShapeDtypeStruct((m, TOP_K), F32),
        ],
        compiler_params=_params("parallel"),
        name="oproj_router",
    )(o, wo, h, g, w_router)


ROW_DMA_UNROLL = 8


def _dispatch_kernel(dest_ref, gend_ref, nu_ref, hn_ref, x_hbm, zero_ref, sem, zsem, *, tm, tile, n_tiles):
    base = pl.program_id(0) * (tm * TOP_K)

    @pl.when(pl.program_id(0) == 0)
    def _():
        zero_ref[...] = jnp.zeros_like(zero_ref)

        def zero_tile(start):
            return pltpu.make_async_copy(zero_ref, x_hbm.at[pl.ds(pl.multiple_of(start, tile), tile), :], zsem)

        def nonempty(e):
            return gend_ref[e] > (gend_ref[e - 1] if e else 0)

        for e in range(N_EXPERTS):
            @pl.when(nonempty(e))
            def _():
                zero_tile(gend_ref[e] - tile).start()

        for e in range(N_EXPERTS):
            @pl.when(nonempty(e))
            def _():
                zero_tile(0).wait()

        def tail_start(j, c):
            zero_tile(j * tile).start()
            return c

        def tail_wait(j, c):
            zero_tile(0).wait()
            return c

        lax.fori_loop(nu_ref[0], n_tiles, tail_start, 0)
        lax.fori_loop(nu_ref[0], n_tiles, tail_wait, 0)

    def row(r, k, dst):
        return pltpu.make_async_copy(hn_ref.at[pl.ds(r, 1), :], x_hbm.at[pl.ds(dst, 1), :], sem)

    def issue(r, c):
        for k in range(TOP_K):
            row(r, k, dest_ref[base + TOP_K * r + k]).start()
        return c

    lax.fori_loop(0, tm, issue, 0, unroll=ROW_DMA_UNROLL)

    def wait(r, c):
        for k in range(TOP_K):
            row(r, k, 0).wait()
        return c

    lax.fori_loop(0, tm, wait, 0, unroll=ROW_DMA_UNROLL)


def moe_dispatch(hn, dest, gend, n_used, padded_rows, *, tm, tile):
    t, d = hn.shape
    grid_spec = pltpu.PrefetchScalarGridSpec(
        num_scalar_prefetch=3,
        grid=(t // tm,),
        in_specs=[pl.BlockSpec((tm, d), lambda i, *_: (i, 0))],
        out_specs=pl.BlockSpec(memory_space=pl.ANY),
        scratch_shapes=[pltpu.VMEM((tile, d), F32), pltpu.SemaphoreType.DMA(()), pltpu.SemaphoreType.DMA(())],
    )
    return pl.pallas_call(
        functools.partial(_dispatch_kernel, tm=tm, tile=tile, n_tiles=padded_rows // tile),
        grid_spec=grid_spec,
        out_shape=jax.ShapeDtypeStruct((padded_rows, d), F32),
        compiler_params=_params("arbitrary"),
        name="moe_dispatch",
    )(dest.reshape(-1), gend, n_used, hn)


def _moe_kernel(te_ref, nu_ref, nv_ref, x_ref, wg_ref, wu_ref, wd_ref, o_ref, xb_ref):
    i = pl.program_id(0)
    f = pl.program_id(1)
    half = x_ref.shape[0] // 2
    n_valid = jnp.where(i < nu_ref[0], nv_ref[i], 0)

    @pl.when(f == 0)
    def _():
        xb_ref[...] = x_ref[...].astype(BF16)
        o_ref[...] = jnp.zeros_like(o_ref)

    @pl.when(n_valid > half)
    def _():
        o_ref[...] += _swiglu_partial(xb_ref[...], wg_ref, wu_ref, wd_ref)

    @pl.when((n_valid > 0) & (n_valid <= half))
    def _():
        o_ref[0:half, :] += _swiglu_partial(xb_ref[0:half, :], wg_ref, wu_ref, wd_ref)


def moe_experts(x, tile_e, n_used, n_valid, w_gate, w_up, w_down, *, tm, tf):
    p, d = x.shape
    nf = w_gate.shape[2] // tf

    def wmap_in(i, f, te, nu, nv):
        return (te[i], 0, jnp.where(i < nu[0], f, nf - 1))

    def wmap_down(i, f, te, nu, nv):
        return (te[i], jnp.where(i < nu[0], f, nf - 1), 0)

    grid_spec = pltpu.PrefetchScalarGridSpec(
        num_scalar_prefetch=3,
        grid=(p // tm, nf),
        in_specs=[
            pl.BlockSpec((tm, d), lambda i, f, *_: (i, 0)),
            pl.BlockSpec((None, d, tf), wmap_in),
            pl.BlockSpec((None, d, tf), wmap_in),
            pl.BlockSpec((None, tf, d), wmap_down),
        ],
        out_specs=pl.BlockSpec((tm, d), lambda i, f, *_: (i, 0)),
        scratch_shapes=[pltpu.VMEM((tm, d), BF16)],
    )
    return pl.pallas_call(
        _moe_kernel,
        grid_spec=grid_spec,
        out_shape=jax.ShapeDtypeStruct((p, d), F32),
        compiler_params=_params("arbitrary", "arbitrary"),
        name="moe_experts",
    )(tile_e, n_used, n_valid, x, w_gate, w_up, w_down)


def _combine_kernel(dest_ref, h_ref, gate_ref, g_ref, y_hbm, o_ref, ybuf, sem, *, tm, n_steps):
    i = pl.program_id(0)

    def row(slot, r, k, src):
        return pltpu.make_async_copy(y_hbm.at[pl.ds(src, 1), :], ybuf.at[slot, k, pl.ds(r, 1), :], sem.at[slot])

    def fetch(step, slot):
        base = step * (tm * TOP_K)

        def issue(r, c):
            for k in range(TOP_K):
                row(slot, r, k, dest_ref[base + TOP_K * r + k]).start()
            return c

        lax.fori_loop(0, tm, issue, 0, unroll=ROW_DMA_UNROLL)

    @pl.when(i == 0)
    def _():
        fetch(0, 0)

    @pl.when(i + 1 < n_steps)
    def _():
        fetch(i + 1, (i + 1) % 2)

    slot = i % 2

    def wait(r, c):
        for k in range(TOP_K):
            row(slot, r, k, 0).wait()
        return c

    lax.fori_loop(0, tm, wait, 0, unroll=ROW_DMA_UNROLL)
    gate = gate_ref[...]
    y = h_ref[...] + gate[:, 0:1] * ybuf[slot, 0] + gate[:, 1:2] * ybuf[slot, 1]
    o_ref[...] = _rms(y, g_ref[...])


def moe_combine_norm(h, gates, y, dest, g, *, tm):
    t, d = h.shape
    n_steps = t // tm
    grid_spec = pltpu.PrefetchScalarGridSpec(
        num_scalar_prefetch=1,
        grid=(n_steps,),
        in_specs=[
            pl.BlockSpec((tm, d), lambda i, dest_ref: (i, 0)),
            pl.BlockSpec((tm, TOP_K), lambda i, dest_ref: (i, 0)),
            pl.BlockSpec((1, d), lambda i, dest_ref: (0, 0)),
            pl.BlockSpec(memory_space=pl.ANY),
        ],
        out_specs=pl.BlockSpec((tm, d), lambda i, dest_ref: (i, 0)),
        scratch_shapes=[pltpu.VMEM((2, TOP_K, tm, d), F32), pltpu.SemaphoreType.DMA((2,))],
    )
    return pl.pallas_call(
        functools.partial(_combine_kernel, tm=tm, n_steps=n_steps),
        grid_spec=grid_spec,
        out_shape=jax.ShapeDtypeStruct((t, d), F32),
        compiler_params=_params("arbitrary"),
        name="moe_combine_norm",
    )(dest.reshape(-1), h, gates, g, y)


def routing_tables(idx, *, tm):
    t = idx.shape[0]
    rows = TOP_K * t
    n_tiles = rows // tm + N_EXPERTS
    e_flat = idx.reshape(rows)
    onehot = (e_flat[:, None] == jnp.arange(N_EXPERTS, dtype=jnp.int32)[None, :]).astype(jnp.int32)
    csum = jnp.cumsum(onehot, axis=0)
    rank = jnp.sum((csum - onehot) * onehot, axis=1)
    counts = csum[-1]
    padded = ((counts + tm - 1) // tm) * tm
    gend = jnp.cumsum(padded)
    gstart = gend - padded
    dest = (jnp.sum(onehot * gstart[None, :], axis=1) + rank).astype(jnp.int32).reshape(t, TOP_K)
    n_used = (gend[-1] // tm).astype(jnp.int32).reshape(1)
    tile_start = jnp.minimum(jnp.arange(n_tiles, dtype=jnp.int32), n_used - 1) * tm
    tile_e = jnp.sum((tile_start[:, None] >= gend[None, :]).astype(jnp.int32), axis=1).astype(jnp.int32)
    n_valid = jnp.clip(jnp.sum((tile_e[:, None] == jnp.arange(N_EXPERTS)[None, :]) * (gstart + counts)[None, :], axis=1)
                       - tile_start, 0, tm).astype(jnp.int32)
    return dest, tile_e, n_used, n_valid, gend.astype(jnp.int32), n_tiles * tm


def kernel(x, meta_tokens, mix_norm_g, ffn_norm_g, conv_pw1_w, conv_pw1_b, conv_dw_w, conv_dw_b,
           conv_ln_g, conv_ln_b, conv_pw2_w, conv_pw2_b, kv_norm_g, w_kv, w_q, w_o, ffn_w_gate,
           ffn_w_up, ffn_w_down, moe_router, moe_w_gate, moe_w_up, moe_w_down, final_norm_g):
    batch, seq, d = x.shape
    t = batch * seq
    row = lambda v: v.reshape(1, -1).astype(F32)

    pw1 = conv_pw1_w[0].astype(BF16)
    pw1_b = row(conv_pw1_b[0])
    dw_w = jnp.pad(conv_dw_w[0], ((0, CONV_HALO - CONV_WIDTH), (0, 0)))
    pw2 = conv_pw2_w[0].astype(BF16)
    w_kvq = jnp.concatenate([w_kv.astype(BF16), w_q[0].astype(BF16)], axis=1)
    g_kvq = jnp.stack([kv_norm_g, mix_norm_g[1]]).astype(F32)
    wo = w_o[0].astype(BF16)
    fg, fu, fd = ffn_w_gate[0].astype(BF16), ffn_w_up[0].astype(BF16), ffn_w_down[0].astype(BF16)
    moe_w = [moe_w_gate[0], moe_w_up[0], moe_w_down[0]]

    def layer0(h, halo0, *, b, l, tm, tl):
        u = rms_glu(h, row(mix_norm_g[0]), pw1, pw1_b, tm=tm, tn=1024)
        h1 = conv_block(u.reshape(b, l, d), halo0, h.reshape(b, l, d), dw_w, row(conv_dw_b[0]),
                        row(conv_ln_g[0]), row(conv_ln_b[0]), pw2, row(conv_pw2_b[0]), tl=tl)
        h2 = ffn_block(h1.reshape(b * l, d), row(ffn_norm_g[0]), fg, fu, fd, tm=tm, tf=512)
        return u, h2

    hm = jnp.pad(meta_tokens.astype(F32), ((0, META_PAD - N_META), (0, 0)))
    um, hm2 = layer0(hm, jnp.zeros((CONV_HALO, d), BF16), b=1, l=META_PAD, tm=META_PAD, tl=META_PAD)
    kv_meta = rms_proj2(hm2, g_kvq, w_kvq, n_cols=2 * D_ATTN, n_first_cols=2 * D_ATTN, second_scale=1.0,
                        tm=META_PAD, tn=1024)
    kv_meta = jnp.pad(kv_meta[:N_META], ((0, META_KEYS - N_META), (0, 0)))

    halo0 = jnp.concatenate([jnp.zeros((CONV_HALO - N_META, d), BF16), um[:N_META]], axis=0)
    _, h2 = layer0(x.reshape(t, d), halo0, b=batch, l=seq, tm=512, tl=512)

    qkv = rms_proj2(h2, g_kvq, w_kvq, n_cols=3 * D_ATTN, n_first_cols=2 * D_ATTN, second_scale=Q_PRESCALE,
                    tm=1024, tn=1024)
    o, moe_bf16 = attention(qkv, kv_meta, [w.reshape(-1, w.shape[-1]) for w in moe_w], batch=batch, seq=seq)
    mg, mu, md = [wb.reshape(w.shape) for wb, w in zip(moe_bf16, moe_w)]
    h3, hn3, idx, gates = oproj_router(o, wo, h2, row(ffn_norm_g[1]), moe_router[0].astype(F32), tm=512)

    moe_tm = 512
    dest, tile_e, n_used, n_valid, gend, padded_rows = routing_tables(idx, tm=moe_tm)
    xs = moe_dispatch(hn3, dest, gend, n_used, padded_rows, tm=512, tile=moe_tm)
    ys = moe_experts(xs, tile_e, n_used, n_valid, mg, mu, md, tm=moe_tm, tf=512)
    out = moe_combine_norm(h3, gates, ys, dest, row(final_norm_g), tm=256)
    return out.reshape(batch, seq, d)
```

```python
import functools
import math

import jax
import jax.numpy as jnp
from jax import lax
from jax.experimental import pallas as pl
from jax.experimental.pallas import tpu as pltpu

D_MODEL = 2048
N_META = 16
META_PAD = 32
N_HEADS = 16
HEAD_DIM = 128
D_ATTN = N_HEADS * HEAD_DIM
CONV_WIDTH = 31
CONV_HALO = 32
D_FF = 5632
N_EXPERTS = 8
TOP_K = 2
RMS_EPS = 1e-6
LN_EPS = 1e-5

VMEM_LIMIT_BYTES = 56 * 1024 * 1024
F32 = jnp.float32
BF16 = jnp.bfloat16
LOG2_E = math.log2(math.e)


def _dot(a, b):
    return jnp.dot(a, b, preferred_element_type=F32)


def _params(*sem):
    return pltpu.CompilerParams(dimension_semantics=sem, vmem_limit_bytes=VMEM_LIMIT_BYTES)


def _resident(shape, index_map):
    return pl.BlockSpec(shape, index_map, pipeline_mode=pl.Buffered(1))


def _rms(h, g):
    var = jnp.mean(h * h, axis=-1, keepdims=True)
    return h * lax.rsqrt(var + RMS_EPS) * g


def _sigmoid(x):
    return 1.0 / (1.0 + jnp.exp(-x))


def _split_bf16(x):
    hi = x.astype(BF16)
    lo = (x - hi.astype(F32)).astype(BF16)
    return hi, lo


def _rms_glu_kernel(h_ref, g_ref, wa_ref, wg_ref, ba_ref, bg_ref, o_ref, hn_ref):
    @pl.when(pl.program_id(1) == 0)
    def _():
        hn_ref[...] = _rms(h_ref[...], g_ref[...]).astype(BF16)

    hn = hn_ref[...]
    half = o_ref.shape[1] // 2
    for sl in (slice(0, half), slice(half, 2 * half)):
        a = _dot(hn, wa_ref[:, sl]) + ba_ref[:, sl]
        g = _dot(hn, wg_ref[:, sl]) + bg_ref[:, sl]
        o_ref[:, sl] = (a * _sigmoid(g)).astype(o_ref.dtype)


def rms_glu(h, g, w, b, *, tm, tn):
    m, d = h.shape
    nb = d // tn
    return pl.pallas_call(
        _rms_glu_kernel,
        grid=(m // tm, nb),
        in_specs=[
            pl.BlockSpec((tm, d), lambda i, n: (i, 0)),
            pl.BlockSpec((1, d), lambda i, n: (0, 0)),
            pl.BlockSpec((d, tn), lambda i, n: (0, n)),
            pl.BlockSpec((d, tn), lambda i, n: (0, n + nb)),
            pl.BlockSpec((1, tn), lambda i, n: (0, n)),
            pl.BlockSpec((1, tn), lambda i, n: (0, n + nb)),
        ],
        out_specs=pl.BlockSpec((tm, tn), lambda i, n: (i, n)),
        out_shape=jax.ShapeDtypeStruct((m, d), BF16),
        scratch_shapes=[pltpu.VMEM((tm, d), BF16)],
        compiler_params=_params("parallel", "arbitrary"),
        name="rms_glu",
    )(h, g, w, w, b, b)


def _rms_proj2_kernel(h_ref, g_ref, w_ref, o_ref, hn_ref, *, n_first, second_scale):
    n = pl.program_id(1)

    @pl.when(n == 0)
    def _():
        h = h_ref[...]
        y = h * lax.rsqrt(jnp.mean(h * h, axis=-1, keepdims=True) + RMS_EPS)
        hn_ref[0] = (y * g_ref[0:1, :]).astype(BF16)
        hn_ref[1] = (y * g_ref[1:2, :]).astype(BF16)

    second = n >= n_first
    r = _dot(hn_ref[second.astype(jnp.int32)], w_ref[...])
    o_ref[...] = (r * jnp.where(second, second_scale, 1.0)).astype(o_ref.dtype)


def rms_proj2(h, g2, w, *, n_cols, n_first_cols, second_scale, tm, tn):
    m, d = h.shape
    return pl.pallas_call(
        functools.partial(_rms_proj2_kernel, n_first=n_first_cols // tn, second_scale=second_scale),
        grid=(m // tm, n_cols // tn),
        in_specs=[
            pl.BlockSpec((tm, d), lambda i, j: (i, 0)),
            pl.BlockSpec((2, d), lambda i, j: (0, 0)),
            pl.BlockSpec((d, tn), lambda i, j: (0, j)),
        ],
        out_specs=pl.BlockSpec((tm, tn), lambda i, j: (i, j)),
        out_shape=jax.ShapeDtypeStruct((m, n_cols), BF16),
        scratch_shapes=[pltpu.VMEM((2, tm, d), BF16)],
        compiler_params=_params("parallel", "arbitrary"),
        name="rms_proj2",
    )(h, g2, w)


CONV_LANES = 256
CONV_ROWS = 64


def _conv_kernel(u_ref, uh_ref, halo0_ref, h_ref, dw_ref, dwb_ref, lng_ref, lnb_ref,
                 w2_ref, b2_ref, o_ref, xp_ref, sh_ref, cv_ref, *, tl):
    i = pl.program_id(1)
    d = u_ref.shape[-1]

    @pl.when(i == 0)
    def _():
        xp_ref[0:CONV_HALO, :] = halo0_ref[...].astype(F32)

    @pl.when(i > 0)
    def _():
        xp_ref[0:CONV_HALO, :] = uh_ref[...].astype(F32)

    xp_ref[CONV_HALO:CONV_HALO + tl, :] = u_ref[...].astype(F32)

    n_sh = tl + CONV_HALO - 8
    rows = min(CONV_ROWS, tl)

    def lane_block(cb, carry):
        c0 = pl.multiple_of(cb * CONV_LANES, CONV_LANES)
        for r in range(1, 8):
            sh_ref[r, 0:n_sh, :] = xp_ref[pl.ds(r, n_sh), pl.ds(c0, CONV_LANES)]

        def row_chunk(rc, carry2):
            r0 = pl.multiple_of(rc * rows, rows)
            acc = jnp.zeros((rows, CONV_LANES), F32)
            for w in range(CONV_WIDTH):
                a, r = divmod(CONV_HALO - (CONV_WIDTH - 1) + w, 8)
                if r == 0:
                    xs = xp_ref[pl.ds(r0 + 8 * a, rows), pl.ds(c0, CONV_LANES)]
                else:
                    xs = sh_ref[r, pl.ds(r0 + 8 * a, rows), :]
                acc = acc + xs * dw_ref[pl.ds(w, 1), pl.ds(c0, CONV_LANES)]
            cv_ref[pl.ds(r0, rows), pl.ds(c0, CONV_LANES)] = acc + dwb_ref[:, pl.ds(c0, CONV_LANES)]
            return carry2

        lax.fori_loop(0, tl // rows, row_chunk, 0)
        return carry

    lax.fori_loop(0, d // CONV_LANES, lane_block, 0)

    c = cv_ref[...]
    mu = jnp.mean(c, axis=-1, keepdims=True)
    cc = c - mu
    var = jnp.mean(cc * cc, axis=-1, keepdims=True)
    y = cc * lax.rsqrt(var + LN_EPS) * lng_ref[...] + lnb_ref[...]
    s = (y * _sigmoid(y)).astype(BF16)
    o_ref[...] = h_ref[...] + _dot(s, w2_ref[...]) + b2_ref[...]


def conv_block(u, halo0, h, dw_w, dw_b, ln_g, ln_b, w2, b2, *, tl):
    b, l, d = u.shape
    hb = tl // CONV_HALO
    return pl.pallas_call(
        functools.partial(_conv_kernel, tl=tl),
        grid=(b, l // tl),
        in_specs=[
            pl.BlockSpec((None, tl, d), lambda bi, i: (bi, i, 0)),
            pl.BlockSpec((None, CONV_HALO, d), lambda bi, i: (bi, jnp.maximum(i * hb - 1, 0), 0)),
            _resident((CONV_HALO, d), lambda bi, i: (0, 0)),
            pl.BlockSpec((None, tl, d), lambda bi, i: (bi, i, 0)),
            _resident((CONV_HALO, d), lambda bi, i: (0, 0)),
            _resident((1, d), lambda bi, i: (0, 0)),
            _resident((1, d), lambda bi, i: (0, 0)),
            _resident((1, d), lambda bi, i: (0, 0)),
            _resident((d, d), lambda bi, i: (0, 0)),
            _resident((1, d), lambda bi, i: (0, 0)),
        ],
        out_specs=pl.BlockSpec((None, tl, d), lambda bi, i: (bi, i, 0)),
        out_shape=jax.ShapeDtypeStruct((b, l, d), F32),
        scratch_shapes=[
            pltpu.VMEM((tl + CONV_HALO, d), F32),
            pltpu.VMEM((8, tl + CONV_HALO, CONV_LANES), F32),
            pltpu.VMEM((tl, d), F32),
        ],
        compiler_params=_params("parallel", "arbitrary"),
        name="conv_block",
    )(u, u, halo0, h, dw_w, dw_b, ln_g, ln_b, w2, b2)


def _swiglu_partial(x, wg_ref, wu_ref, wd_ref):
    half = wg_ref.shape[1] // 2
    acts = []
    for sl in (slice(0, half), slice(half, 2 * half)):
        a = _dot(x, wg_ref[:, sl])
        u = _dot(x, wu_ref[:, sl])
        acts.append((a * _sigmoid(a) * u).astype(BF16))
    return _dot(jnp.concatenate(acts, axis=1), wd_ref[...])


def _ffn_kernel(h_ref, g_ref, wg_ref, wu_ref, wd_ref, o_ref, hn_ref):
    @pl.when(pl.program_id(1) == 0)
    def _():
        h = h_ref[...]
        hn_ref[...] = _rms(h, g_ref[...]).astype(BF16)
        o_ref[...] = h

    o_ref[...] += _swiglu_partial(hn_ref[...], wg_ref, wu_ref, wd_ref)


def ffn_block(h, g, w_gate, w_up, w_down, *, tm, tf):
    m, d = h.shape
    ff = w_gate.shape[1]
    return pl.pallas_call(
        _ffn_kernel,
        grid=(m // tm, ff // tf),
        in_specs=[
            pl.BlockSpec((tm, d), lambda i, f: (i, 0)),
            pl.BlockSpec((1, d), lambda i, f: (0, 0)),
            pl.BlockSpec((d, tf), lambda i, f: (0, f)),
            pl.BlockSpec((d, tf), lambda i, f: (0, f)),
            pl.BlockSpec((tf, d), lambda i, f: (f, 0)),
        ],
        out_specs=pl.BlockSpec((tm, d), lambda i, f: (i, 0)),
        out_shape=jax.ShapeDtypeStruct((m, d), F32),
        scratch_shapes=[pltpu.VMEM((tm, d), BF16)],
        compiler_params=_params("parallel", "arbitrary"),
        name="ffn_block",
    )(h, g, w_gate, w_up, w_down)


ATTN_TILE = 256
ATTN_HEADS = 4
META_KEYS = 128
Q_PRESCALE = -(HEAD_DIM ** -0.5) * LOG2_E


def _attn_kernel(q_ref, k_ref, v_ref, km_ref, vm_ref, *rest, seq, n_cast):
    cast_src = rest[:n_cast]
    o_ref = rest[n_cast]
    cast_dst = rest[n_cast + 1:2 * n_cast + 1]
    cast_in = rest[2 * n_cast + 1:3 * n_cast + 1]
    cast_out = rest[3 * n_cast + 1:4 * n_cast + 1]
    sem_in, sem_out = rest[4 * n_cast + 1:]
    t = ATTN_TILE
    nq = seq // t
    step = pl.program_id(0) * pl.num_programs(1) + pl.program_id(1)

    def chunk_rows(a, chunk):
        rows = cast_in[a].shape[1]
        return pl.ds(pl.multiple_of(chunk * rows, 16), rows)

    def cast_loads(chunk, slot):
        return [pltpu.make_async_copy(cast_src[a].at[chunk_rows(a, chunk), :], cast_in[a].at[slot], sem_in.at[slot, a])
                for a in range(n_cast)]

    def cast_stores(chunk, slot):
        return [pltpu.make_async_copy(cast_out[a].at[slot], cast_dst[a].at[chunk_rows(a, chunk), :], sem_out.at[slot, a])
                for a in range(n_cast)]

    row = lax.broadcasted_iota(jnp.int32, (t, t), 0)
    col = lax.broadcasted_iota(jnp.int32, (t, t), 1)
    later = (row > col).astype(BF16)
    later_m = later[:META_KEYS, :META_KEYS]
    causal = col < row
    meta_ok = lax.broadcasted_iota(jnp.int32, (t, META_KEYS), 1) < N_META

    heads = [slice(hh * HEAD_DIM, (hh + 1) * HEAD_DIM) for hh in range(ATTN_HEADS)]

    def scores(qts, kts, later_mat, mask):
        ys = [lax.dot_general(qt, kt, (((1,), (1,)), ((), ())), preferred_element_type=F32)
              for qt, kt in zip(qts, kts)]
        sums, parts, own = [], [], []
        for y in ys:
            e = jnp.exp2(jnp.minimum(y, -y))
            ls = jnp.minimum(y, 0.0) - jnp.log2(1.0 + e)
            if mask is not None:
                ls = jnp.where(mask, ls, 0.0)
            sums.append(jnp.sum(ls, axis=-1, keepdims=True))
            parts.append(ls.astype(BF16))
            own.append(ls - y)
        logws = []
        for part, o in zip(parts, own):
            logw = o + _dot(part, later_mat)
            if mask is not None:
                logw = jnp.where(mask, logw, -jnp.inf)
            logws.append(logw)
        return sums, logws

    def weighted_values(logws, runs, vts):
        return [_dot(jnp.exp2(logw + run).astype(BF16), vt) for logw, run, vt in zip(logws, runs, vts)]

    def q_loop(i, carry):
        chunk = step * nq + i
        slot = i % 2

        @pl.when(i == 0)
        def _():
            for dma in cast_loads(chunk, slot):
                dma.start()

        @pl.when(i + 1 < nq)
        def _():
            for dma in cast_loads(chunk + 1, 1 - slot):
                dma.start()

        q0 = pl.multiple_of(i * t, t)
        qts = [q_ref[pl.ds(q0, t), sl] for sl in heads]

        def kv(k0):
            return [k_ref[pl.ds(k0, t), sl] for sl in heads], [v_ref[pl.ds(k0, t), sl] for sl in heads]

        kts, vts = kv(q0)
        runs, logws = scores(qts, kts, later, causal)
        _, logws_meta = scores(qts, [km_ref[:, sl] for sl in heads], later_m, meta_ok)
        accs = weighted_values(logws, [0.0] * ATTN_HEADS, vts)

        def kv_loop(jj, c):
            runs, accs = c
            kts, vts = kv(pl.multiple_of((i - 1 - jj) * t, t))
            sums, logws = scores(qts, kts, later, None)
            pvs = weighted_values(logws, runs, vts)
            return [run + s for run, s in zip(runs, sums)], [acc + pv for acc, pv in zip(accs, pvs)]

        runs, accs = lax.fori_loop(0, i, kv_loop, (runs, accs))
        pvs = weighted_values(logws_meta, runs, [vm_ref[:, sl] for sl in heads])
        for sl, acc, pv in zip(heads, accs, pvs):
            o_ref[pl.ds(q0, t), sl] = (acc + pv).astype(o_ref.dtype)

        for dma in cast_loads(chunk, slot):
            dma.wait()

        @pl.when(i >= 2)
        def _():
            for dma in cast_stores(chunk - 2, slot):
                dma.wait()

        for a in range(n_cast):
            cast_out[a][slot] = cast_in[a][slot].astype(BF16)
        for dma in cast_stores(chunk, slot):
            dma.start()
        return carry

    lax.fori_loop(0, nq, q_loop, 0)
    for back in (2, 1):
        for dma in cast_stores(step * nq + nq - back, (nq - back) % 2):
            dma.wait()


def attention(qkv, kv_meta, cast_srcs, *, batch, seq):
    w = ATTN_HEADS * HEAD_DIM
    nb = D_ATTN // w
    n_cast = len(cast_srcs)
    n_chunks = batch * nb * (seq // ATTN_TILE)
    chunk_shapes = [(a.shape[0] // n_chunks, a.shape[1]) for a in cast_srcs]
    any_spec = pl.BlockSpec(memory_space=pl.ANY)
    outs = pl.pallas_call(
        functools.partial(_attn_kernel, seq=seq, n_cast=n_cast),
        grid=(batch, nb),
        in_specs=[
            pl.BlockSpec((seq, w), lambda b, hp: (b, 2 * nb + hp)),
            pl.BlockSpec((seq, w), lambda b, hp: (b, hp)),
            pl.BlockSpec((seq, w), lambda b, hp: (b, nb + hp)),
            pl.BlockSpec((META_KEYS, w), lambda b, hp: (0, hp)),
            pl.BlockSpec((META_KEYS, w), lambda b, hp: (0, nb + hp)),
        ] + [any_spec] * n_cast,
        out_specs=[pl.BlockSpec((seq, w), lambda b, hp: (b, hp))] + [any_spec] * n_cast,
        out_shape=[jax.ShapeDtypeStruct((batch * seq, D_ATTN), BF16)]
        + [jax.ShapeDtypeStruct(a.shape, BF16) for a in cast_srcs],
        scratch_shapes=[pltpu.VMEM((2,) + cs, F32) for cs in chunk_shapes]
        + [pltpu.VMEM((2,) + cs, BF16) for cs in chunk_shapes]
        + [pltpu.SemaphoreType.DMA((2, n_cast)), pltpu.SemaphoreType.DMA((2, n_cast))],
        compiler_params=_params("parallel", "parallel"),
        name="attention",
    )(qkv, qkv, qkv, kv_meta, kv_meta, *cast_srcs)
    return outs[0], outs[1:]


def _oproj_router_kernel(o_ref, wo_ref, h_ref, g_ref, wr_ref, h3_ref, hn_ref, idx_ref, gate_ref):
    h3 = h_ref[...] + _dot(o_ref[...], wo_ref[...])
    h3_ref[...] = h3
    hn = _rms(h3, g_ref[...])
    hn_ref[...] = hn
    hn_hi, hn_lo = _split_bf16(hn)
    wr_hi, wr_lo = _split_bf16(wr_ref[...])
    logits = _dot(hn_hi, wr_hi) + (_dot(hn_hi, wr_lo) + _dot(hn_lo, wr_hi))
    e = lax.broadcasted_iota(jnp.int32, logits.shape, 1)
    m1 = jnp.max(logits, axis=-1, keepdims=True)
    i1 = jnp.min(jnp.where(logits == m1, e, N_EXPERTS), axis=-1, keepdims=True)
    rest = jnp.where(e == i1, -jnp.inf, logits)
    m2 = jnp.max(rest, axis=-1, keepdims=True)
    i2 = jnp.min(jnp.where(rest == m2, e, N_EXPERTS), axis=-1, keepdims=True)
    p = jnp.exp(m2 - m1)
    g1 = 1.0 / (1.0 + p)
    k = lax.broadcasted_iota(jnp.int32, idx_ref.shape, 1)
    idx_ref[...] = jnp.where(k == 0, i1, i2)
    gate_ref[...] = jnp.where(k == 0, g1, p * g1)


def oproj_router(o, wo, h, g, w_router, *, tm):
    m, d = h.shape
    return pl.pallas_call(
        _oproj_router_kernel,
        grid=(m // tm,),
        in_specs=[
            pl.BlockSpec((tm, d), lambda i: (i, 0)),
            _resident((d, d), lambda i: (0, 0)),
            pl.BlockSpec((tm, d), lambda i: (i, 0)),
            _resident((1, d), lambda i: (0, 0)),
            _resident((d, N_EXPERTS), lambda i: (0, 0)),
        ],
        out_specs=[
            pl.BlockSpec((tm, d), lambda i: (i, 0)),
            pl.BlockSpec((tm, d), lambda i: (i, 0)),
            pl.BlockSpec((tm, TOP_K), lambda i: (i, 0)),
            pl.BlockSpec((tm, TOP_K), lambda i: (i, 0)),
        ],
        out_shape=[
            jax.ShapeDtypeStruct((m, d), F32),
            jax.ShapeDtypeStruct((m, d), F32),
            jax.ShapeDtypeStruct((m, TOP_K), jnp.int32),
            jax.ShapeDtypeStruct((m, TOP_K), F32),
        ],
        compiler_params=_params("parallel"),
        name="oproj_router",
    )(o, wo, h, g, w_router)


ROW_DMA_UNROLL = 8


def _dispatch_kernel(dest_ref, gend_ref, nu_ref, hn_ref, x_hbm, zero_ref, sem, zsem, *, tm, tile, n_tiles):
    base = pl.program_id(0) * (tm * TOP_K)

    @pl.when(pl.program_id(0) == 0)
    def _():
        zero_ref[...] = jnp.zeros_like(zero_ref)

        def zero_tile(start):
            return pltpu.make_async_copy(zero_ref, x_hbm.at[pl.ds(pl.multiple_of(start, tile), tile), :], zsem)

        def nonempty(e):
            return gend_ref[e] > (gend_ref[e - 1] if e else 0)

        for e in range(N_EXPERTS):
            @pl.when(nonempty(e))
            def _():
                zero_tile(gend_ref[e] - tile).start()

        for e in range(N_EXPERTS):
            @pl.when(nonempty(e))
            def _():
                zero_tile(0).wait()

        def tail_start(j, c):
            zero_tile(j * tile).start()
            return c

        def tail_wait(j, c):
            zero_tile(0).wait()
            return c

        lax.fori_loop(nu_ref[0], n_tiles, tail_start, 0)
        lax.fori_loop(nu_ref[0], n_tiles, tail_wait, 0)

    def row(r, k, dst):
        return pltpu.make_async_copy(hn_ref.at[pl.ds(r, 1), :], x_hbm.at[pl.ds(dst, 1), :], sem)

    def issue(r, c):
        for k in range(TOP_K):
            row(r, k, dest_ref[base + TOP_K * r + k]).start(priority=k)
        return c

    lax.fori_loop(0, tm, issue, 0, unroll=ROW_DMA_UNROLL)

    def wait(r, c):
        for k in range(TOP_K):
            row(r, k, 0).wait()
        return c

    lax.fori_loop(0, tm, wait, 0, unroll=ROW_DMA_UNROLL)


def moe_dispatch(hn, dest, gend, n_used, padded_rows, *, tm, tile):
    t, d = hn.shape
    grid_spec = pltpu.PrefetchScalarGridSpec(
        num_scalar_prefetch=3,
        grid=(t // tm,),
        in_specs=[pl.BlockSpec((tm, d), lambda i, *_: (i, 0))],
        out_specs=pl.BlockSpec(memory_space=pl.ANY),
        scratch_shapes=[pltpu.VMEM((tile, d), F32), pltpu.SemaphoreType.DMA(()), pltpu.SemaphoreType.DMA(())],
    )
    return pl.pallas_call(
        functools.partial(_dispatch_kernel, tm=tm, tile=tile, n_tiles=padded_rows // tile),
        grid_spec=grid_spec,
        out_shape=jax.ShapeDtypeStruct((padded_rows, d), F32),
        compiler_params=_params("arbitrary"),
        name="moe_dispatch",
    )(dest.reshape(-1), gend, n_used, hn)


def _moe_kernel(te_ref, nu_ref, nv_ref, x_ref, wg_ref, wu_ref, wd_ref, o_ref, xb_ref):
    i = pl.program_id(0)
    f = pl.program_id(1)
    half = x_ref.shape[0] // 2
    n_valid = jnp.where(i < nu_ref[0], nv_ref[i], 0)

    @pl.when(f == 0)
    def _():
        xb_ref[...] = x_ref[...].astype(BF16)
        o_ref[...] = jnp.zeros_like(o_ref)

    @pl.when(n_valid > half)
    def _():
        o_ref[...] += _swiglu_partial(xb_ref[...], wg_ref, wu_ref, wd_ref)

    @pl.when((n_valid > 0) & (n_valid <= half))
    def _():
        o_ref[0:half, :] += _swiglu_partial(xb_ref[0:half, :], wg_ref, wu_ref, wd_ref)


def moe_experts(x, tile_e, n_used, n_valid, w_gate, w_up, w_down, *, tm, tf):
    p, d = x.shape
    nf = w_gate.shape[2] // tf

    def wmap_in(i, f, te, nu, nv):
        return (te[i], 0, jnp.where(i < nu[0], f, nf - 1))

    def wmap_down(i, f, te, nu, nv):
        return (te[i], jnp.where(i < nu[0], f, nf - 1), 0)

    grid_spec = pltpu.PrefetchScalarGridSpec(
        num_scalar_prefetch=3,
        grid=(p // tm, nf),
        in_specs=[
            pl.BlockSpec((tm, d), lambda i, f, *_: (i, 0)),
            pl.BlockSpec((None, d, tf), wmap_in),
            pl.BlockSpec((None, d, tf), wmap_in),
            pl.BlockSpec((None, tf, d), wmap_down),
        ],
        out_specs=pl.BlockSpec((tm, d), lambda i, f, *_: (i, 0)),
        scratch_shapes=[pltpu.VMEM((tm, d), BF16)],
    )
    return pl.pallas_call(
        _moe_kernel,
        grid_spec=grid_spec,
        out_shape=jax.ShapeDtypeStruct((p, d), F32),
        compiler_params=_params("arbitrary", "arbitrary"),
        name="moe_experts",
    )(tile_e, n_used, n_valid, x, w_gate, w_up, w_down)


def _combine_kernel(dest_ref, h_ref, gate_ref, g_ref, y_hbm, o_ref, ybuf, sem, *, tm, n_steps):
    i = pl.program_id(0)

    def row(slot, r, k, src):
        return pltpu.make_async_copy(y_hbm.at[pl.ds(src, 1), :], ybuf.at[slot, k, pl.ds(r, 1), :], sem.at[slot])

    def fetch(step, slot):
        base = step * (tm * TOP_K)

        def issue(r, c):
            for k in range(TOP_K):
                row(slot, r, k, dest_ref[base + TOP_K * r + k]).start(priority=k)
            return c

        lax.fori_loop(0, tm, issue, 0, unroll=ROW_DMA_UNROLL)

    @pl.when(i == 0)
    def _():
        fetch(0, 0)

    @pl.when(i + 1 < n_steps)
    def _():
        fetch(i + 1, (i + 1) % 2)

    slot = i % 2

    def wait(r, c):
        for k in range(TOP_K):
            row(slot, r, k, 0).wait()
        return c

    lax.fori_loop(0, tm, wait, 0, unroll=ROW_DMA_UNROLL)
    gate = gate_ref[...]
    y = h_ref[...] + gate[:, 0:1] * ybuf[slot, 0] + gate[:, 1:2] * ybuf[slot, 1]
    o_ref[...] = _rms(y, g_ref[...])


def moe_combine_norm(h, gates, y, dest, g, *, tm):
    t, d = h.shape
    n_steps = t // tm
    grid_spec = pltpu.PrefetchScalarGridSpec(
        num_scalar_prefetch=1,
        grid=(n_steps,),
        in_specs=[
            pl.BlockSpec((tm, d), lambda i, dest_ref: (i, 0)),
            pl.BlockSpec((tm, TOP_K), lambda i, dest_ref: (i, 0)),
            pl.BlockSpec((1, d), lambda i, dest_ref: (0, 0)),
            pl.BlockSpec(memory_space=pl.ANY),
        ],
        out_specs=pl.BlockSpec((tm, d), lambda i, dest_ref: (i, 0)),
        scratch_shapes=[pltpu.VMEM((2, TOP_K, tm, d), F32), pltpu.SemaphoreType.DMA((2,))],
    )
    return pl.pallas_call(
        functools.partial(_combine_kernel, tm=tm, n_steps=n_steps),
        grid_spec=grid_spec,
        out_shape=jax.ShapeDtypeStruct((t, d), F32),
        compiler_params=_params("arbitrary"),
        name="moe_combine_norm",
    )(dest.reshape(-1), h, gates, g, y)


def routing_tables(idx, *, tm):
    t = idx.shape[0]
    rows = TOP_K * t
    n_tiles = rows // tm + N_EXPERTS
    e_flat = idx.reshape(rows)
    onehot = (e_flat[:, None] == jnp.arange(N_EXPERTS, dtype=jnp.int32)[None, :]).astype(jnp.int32)
    csum = jnp.cumsum(onehot, axis=0)
    rank = jnp.sum((csum - onehot) * onehot, axis=1)
    counts = csum[-1]
    padded = ((counts + tm - 1) // tm) * tm
    gend = jnp.cumsum(padded)
    gstart = gend - padded
    dest = (jnp.sum(onehot * gstart[None, :], axis=1) + rank).astype(jnp.int32).reshape(t, TOP_K)
    n_used = (gend[-1] // tm).astype(jnp.int32).reshape(1)
    tile_start = jnp.minimum(jnp.arange(n_tiles, dtype=jnp.int32), n_used - 1) * tm
    tile_e = jnp.sum((tile_start[:, None] >= gend[None, :]).astype(jnp.int32), axis=1).astype(jnp.int32)
    n_valid = jnp.clip(jnp.sum((tile_e[:, None] == jnp.arange(N_EXPERTS)[None, :]) * (gstart + counts)[None, :], axis=1)
                       - tile_start, 0, tm).astype(jnp.int32)
    return dest, tile_e, n_used, n_valid, gend.astype(jnp.int32), n_tiles * tm


def kernel(x, meta_tokens, mix_norm_g, ffn_norm_g, conv_pw1_w, conv_pw1_b, conv_dw_w, conv_dw_b,
           conv_ln_g, conv_ln_b, conv_pw2_w, conv_pw2_b, kv_norm_g, w_kv, w_q, w_o, ffn_w_gate,
           ffn_w_up, ffn_w_down, moe_router, moe_w_gate, moe_w_up, moe_w_down, final_norm_g):
    batch, seq, d = x.shape
    t = batch * seq
    row = lambda v: v.reshape(1, -1).astype(F32)

    pw1 = conv_pw1_w[0].astype(BF16)
    pw1_b = row(conv_pw1_b[0])
    dw_w = jnp.pad(conv_dw_w[0], ((0, CONV_HALO - CONV_WIDTH), (0, 0)))
    pw2 = conv_pw2_w[0].astype(BF16)
    w_kvq = jnp.concatenate([w_kv.astype(BF16), w_q[0].astype(BF16)], axis=1)
    g_kvq = jnp.stack([kv_norm_g, mix_norm_g[1]]).astype(F32)
    wo = w_o[0].astype(BF16)
    fg, fu, fd = ffn_w_gate[0].astype(BF16), ffn_w_up[0].astype(BF16), ffn_w_down[0].astype(BF16)
    moe_w = [moe_w_gate[0], moe_w_up[0], moe_w_down[0]]

    def layer0(h, halo0, *, b, l, tm, tl):
        u = rms_glu(h, row(mix_norm_g[0]), pw1, pw1_b, tm=min(2 * tm, b * l), tn=1024)
        h1 = conv_block(u.reshape(b, l, d), halo0, h.reshape(b, l, d), dw_w, row(conv_dw_b[0]),
                        row(conv_ln_g[0]), row(conv_ln_b[0]), pw2, row(conv_pw2_b[0]), tl=tl)
        h2 = ffn_block(h1.reshape(b * l, d), row(ffn_norm_g[0]), fg, fu, fd, tm=min(2 * tm, b * l), tf=512)
        return u, h2

    hm = jnp.pad(meta_tokens.astype(F32), ((0, META_PAD - N_META), (0, 0)))
    um, hm2 = layer0(hm, jnp.zeros((CONV_HALO, d), BF16), b=1, l=META_PAD, tm=META_PAD, tl=META_PAD)
    kv_meta = rms_proj2(hm2, g_kvq, w_kvq, n_cols=2 * D_ATTN, n_first_cols=2 * D_ATTN, second_scale=1.0,
                        tm=META_PAD, tn=1024)
    kv_meta = jnp.pad(kv_meta[:N_META], ((0, META_KEYS - N_META), (0, 0)))

    halo0 = jnp.concatenate([jnp.zeros((CONV_HALO - N_META, d), BF16), um[:N_META]], axis=0)
    _, h2 = layer0(x.reshape(t, d), halo0, b=batch, l=seq, tm=512, tl=512)

    qkv = rms_proj2(h2, g_kvq, w_kvq, n_cols=3 * D_ATTN, n_first_cols=2 * D_ATTN, second_scale=Q_PRESCALE,
                    tm=1024, tn=1024)
    o, moe_bf16 = attention(qkv, kv_meta, [w.reshape(-1, w.shape[-1]) for w in moe_w], batch=batch, seq=seq)
    mg, mu, md = [wb.reshape(w.shape) for wb, w in zip(moe_bf16, moe_w)]
    h3, hn3, idx, gates = oproj_router(o, wo, h2, row(ffn_norm_g[1]), moe_router[0].astype(F32), tm=512)

    moe_tm = 512
    dest, tile_e, n_used, n_valid, gend, padded_rows = routing_tables(idx, tm=moe_tm)
    xs = moe_dispatch(hn3, dest, gend, n_used, padded_rows, tm=512, tile=moe_tm)
    ys = moe_experts(xs, tile_e, n_used, n_valid, mg, mu, md, tm=moe_tm, tf=512)
    out = moe_combine_norm(h3, gates, ys, dest, row(final_norm_g), tm=256)
    return out.reshape(batch, seq, d)
```

```python
import functools
import math

import jax
import jax.numpy as jnp
from jax import lax
from jax.experimental import pallas as pl
from jax.experimental.pallas import tpu as pltpu

D_MODEL = 2048
N_META = 16
META_PAD = 32
N_HEADS = 16
HEAD_DIM = 128
D_ATTN = N_HEADS * HEAD_DIM
CONV_WIDTH = 31
CONV_HALO = 32
D_FF = 5632
N_EXPERTS = 8
TOP_K = 2
RMS_EPS = 1e-6
LN_EPS = 1e-5

VMEM_LIMIT_BYTES = 56 * 1024 * 1024
F32 = jnp.float32
BF16 = jnp.bfloat16
LOG2_E = math.log2(math.e)


def _dot(a, b):
    return jnp.dot(a, b, preferred_element_type=F32)


def _params(*sem):
    return pltpu.CompilerParams(dimension_semantics=sem, vmem_limit_bytes=VMEM_LIMIT_BYTES)


def _resident(shape, index_map):
    return pl.BlockSpec(shape, index_map, pipeline_mode=pl.Buffered(1))


def _rms(h, g):
    var = jnp.mean(h * h, axis=-1, keepdims=True)
    return h * lax.rsqrt(var + RMS_EPS) * g


def _sigmoid(x):
    return 1.0 / (1.0 + jnp.exp(-x))


def _split_bf16(x):
    hi = x.astype(BF16)
    lo = (x - hi.astype(F32)).astype(BF16)
    return hi, lo


def _rms_glu_kernel(h_ref, g_ref, wa_ref, wg_ref, ba_ref, bg_ref, o_ref, hn_ref):
    @pl.when(pl.program_id(1) == 0)
    def _():
        hn_ref[...] = _rms(h_ref[...], g_ref[...]).astype(BF16)

    hn = hn_ref[...]
    half = o_ref.shape[1] // 2
    for sl in (slice(0, half), slice(half, 2 * half)):
        a = _dot(hn, wa_ref[:, sl]) + ba_ref[:, sl]
        g = _dot(hn, wg_ref[:, sl]) + bg_ref[:, sl]
        o_ref[:, sl] = (a * _sigmoid(g)).astype(o_ref.dtype)


def rms_glu(h, g, w, b, *, tm, tn):
    m, d = h.shape
    nb = d // tn
    return pl.pallas_call(
        _rms_glu_kernel,
        grid=(m // tm, nb),
        in_specs=[
            pl.BlockSpec((tm, d), lambda i, n: (i, 0)),
            pl.BlockSpec((1, d), lambda i, n: (0, 0)),
            pl.BlockSpec((d, tn), lambda i, n: (0, n)),
            pl.BlockSpec((d, tn), lambda i, n: (0, n + nb)),
            pl.BlockSpec((1, tn), lambda i, n: (0, n)),
            pl.BlockSpec((1, tn), lambda i, n: (0, n + nb)),
        ],
        out_specs=pl.BlockSpec((tm, tn), lambda i, n: (i, n)),
        out_shape=jax.ShapeDtypeStruct((m, d), BF16),
        scratch_shapes=[pltpu.VMEM((tm, d), BF16)],
        compiler_params=_params("parallel", "arbitrary"),
        name="rms_glu",
    )(h, g, w, w, b, b)


def _rms_proj2_kernel(h_ref, g_ref, w_ref, o_ref, hn_ref, *, n_first, second_scale):
    n = pl.program_id(1)

    @pl.when(n == 0)
    def _():
        h = h_ref[...]
        y = h * lax.rsqrt(jnp.mean(h * h, axis=-1, keepdims=True) + RMS_EPS)
        hn_ref[0] = (y * g_ref[0:1, :]).astype(BF16)
        hn_ref[1] = (y * g_ref[1:2, :]).astype(BF16)

    second = n >= n_first
    r = _dot(hn_ref[second.astype(jnp.int32)], w_ref[...])
    o_ref[...] = (r * jnp.where(second, second_scale, 1.0)).astype(o_ref.dtype)


def rms_proj2(h, g2, w, *, n_cols, n_first_cols, second_scale, tm, tn):
    m, d = h.shape
    return pl.pallas_call(
        functools.partial(_rms_proj2_kernel, n_first=n_first_cols // tn, second_scale=second_scale),
        grid=(m // tm, n_cols // tn),
        in_specs=[
            pl.BlockSpec((tm, d), lambda i, j: (i, 0)),
            pl.BlockSpec((2, d), lambda i, j: (0, 0)),
            pl.BlockSpec((d, tn), lambda i, j: (0, j)),
        ],
        out_specs=pl.BlockSpec((tm, tn), lambda i, j: (i, j)),
        out_shape=jax.ShapeDtypeStruct((m, n_cols), BF16),
        scratch_shapes=[pltpu.VMEM((2, tm, d), BF16)],
        compiler_params=_params("parallel", "arbitrary"),
        name="rms_proj2",
    )(h, g2, w)


CONV_LANES = 256
CONV_ROWS = 64


def _conv_kernel(u_ref, uh_ref, halo0_ref, h_ref, dw_ref, dwb_ref, lng_ref, lnb_ref,
                 w2_ref, b2_ref, o_ref, xp_ref, sh_ref, cv_ref, *, tl):
    i = pl.program_id(1)
    d = u_ref.shape[-1]

    @pl.when(i == 0)
    def _():
        xp_ref[0:CONV_HALO, :] = halo0_ref[...].astype(F32)

    @pl.when(i > 0)
    def _():
        xp_ref[0:CONV_HALO, :] = uh_ref[...].astype(F32)

    xp_ref[CONV_HALO:CONV_HALO + tl, :] = u_ref[...].astype(F32)

    n_sh = tl + CONV_HALO - 8
    rows = min(CONV_ROWS, tl)

    def lane_block(cb, carry):
        c0 = pl.multiple_of(cb * CONV_LANES, CONV_LANES)
        for r in range(1, 8):
            sh_ref[r, 0:n_sh, :] = xp_ref[pl.ds(r, n_sh), pl.ds(c0, CONV_LANES)]

        def row_chunk(rc, carry2):
            r0 = pl.multiple_of(rc * rows, rows)
            acc = jnp.zeros((rows, CONV_LANES), F32)
            for w in range(CONV_WIDTH):
                a, r = divmod(CONV_HALO - (CONV_WIDTH - 1) + w, 8)
                if r == 0:
                    xs = xp_ref[pl.ds(r0 + 8 * a, rows), pl.ds(c0, CONV_LANES)]
                else:
                    xs = sh_ref[r, pl.ds(r0 + 8 * a, rows), :]
                acc = acc + xs * dw_ref[pl.ds(w, 1), pl.ds(c0, CONV_LANES)]
            cv_ref[pl.ds(r0, rows), pl.ds(c0, CONV_LANES)] = acc + dwb_ref[:, pl.ds(c0, CONV_LANES)]
            return carry2

        lax.fori_loop(0, tl // rows, row_chunk, 0)
        return carry

    lax.fori_loop(0, d // CONV_LANES, lane_block, 0)

    c = cv_ref[...]
    mu = jnp.mean(c, axis=-1, keepdims=True)
    cc = c - mu
    var = jnp.mean(cc * cc, axis=-1, keepdims=True)
    y = cc * lax.rsqrt(var + LN_EPS) * lng_ref[...] + lnb_ref[...]
    s = (y * _sigmoid(y)).astype(BF16)
    o_ref[...] = h_ref[...] + _dot(s, w2_ref[...]) + b2_ref[...]


def conv_block(u, halo0, h, dw_w, dw_b, ln_g, ln_b, w2, b2, *, tl):
    b, l, d = u.shape
    hb = tl // CONV_HALO
    return pl.pallas_call(
        functools.partial(_conv_kernel, tl=tl),
        grid=(b, l // tl),
        in_specs=[
            pl.BlockSpec((None, tl, d), lambda bi, i: (bi, i, 0)),
            pl.BlockSpec((None, CONV_HALO, d), lambda bi, i: (bi, jnp.maximum(i * hb - 1, 0), 0)),
            _resident((CONV_HALO, d), lambda bi, i: (0, 0)),
            pl.BlockSpec((None, tl, d), lambda bi, i: (bi, i, 0)),
            _resident((CONV_HALO, d), lambda bi, i: (0, 0)),
            _resident((1, d), lambda bi, i: (0, 0)),
            _resident((1, d), lambda bi, i: (0, 0)),
            _resident((1, d), lambda bi, i: (0, 0)),
            _resident((d, d), lambda bi, i: (0, 0)),
            _resident((1, d), lambda bi, i: (0, 0)),
        ],
        out_specs=pl.BlockSpec((None, tl, d), lambda bi, i: (bi, i, 0)),
        out_shape=jax.ShapeDtypeStruct((b, l, d), F32),
        scratch_shapes=[
            pltpu.VMEM((tl + CONV_HALO, d), F32),
            pltpu.VMEM((8, tl + CONV_HALO, CONV_LANES), F32),
            pltpu.VMEM((tl, d), F32),
        ],
        compiler_params=_params("parallel", "arbitrary"),
        name="conv_block",
    )(u, u, halo0, h, dw_w, dw_b, ln_g, ln_b, w2, b2)


def _swiglu_partial(x, wg_ref, wu_ref, wd_ref):
    half = wg_ref.shape[1] // 2
    acts = []
    for sl in (slice(0, half), slice(half, 2 * half)):
        a = _dot(x, wg_ref[:, sl])
        u = _dot(x, wu_ref[:, sl])
        acts.append((a * _sigmoid(a) * u).astype(BF16))
    return _dot(jnp.concatenate(acts, axis=1), wd_ref[...])


def _ffn_kernel(h_ref, g_ref, wg_ref, wu_ref, wd_ref, o_ref, hn_ref):
    @pl.when(pl.program_id(1) == 0)
    def _():
        h = h_ref[...]
        hn_ref[...] = _rms(h, g_ref[...]).astype(BF16)
        o_ref[...] = h

    o_ref[...] += _swiglu_partial(hn_ref[...], wg_ref, wu_ref, wd_ref)


def ffn_block(h, g, w_gate, w_up, w_down, *, tm, tf):
    m, d = h.shape
    ff = w_gate.shape[1]
    return pl.pallas_call(
        _ffn_kernel,
        grid=(m // tm, ff // tf),
        in_specs=[
            pl.BlockSpec((tm, d), lambda i, f: (i, 0)),
            pl.BlockSpec((1, d), lambda i, f: (0, 0)),
            pl.BlockSpec((d, tf), lambda i, f: (0, f)),
            pl.BlockSpec((d, tf), lambda i, f: (0, f)),
            pl.BlockSpec((tf, d), lambda i, f: (f, 0)),
        ],
        out_specs=pl.BlockSpec((tm, d), lambda i, f: (i, 0)),
        out_shape=jax.ShapeDtypeStruct((m, d), F32),
        scratch_shapes=[pltpu.VMEM((tm, d), BF16)],
        compiler_params=_params("parallel", "arbitrary"),
        name="ffn_block",
    )(h, g, w_gate, w_up, w_down)


ATTN_TILE = 256
ATTN_HEADS = 4
META_KEYS = 128
Q_PRESCALE = -(HEAD_DIM ** -0.5) * LOG2_E


def _attn_kernel(q_ref, k_ref, v_ref, km_ref, vm_ref, *rest, seq, n_cast):
    cast_src = rest[:n_cast]
    o_ref = rest[n_cast]
    cast_dst = rest[n_cast + 1:2 * n_cast + 1]
    cast_in = rest[2 * n_cast + 1:3 * n_cast + 1]
    cast_out = rest[3 * n_cast + 1:4 * n_cast + 1]
    sem_in, sem_out = rest[4 * n_cast + 1:]
    t = ATTN_TILE
    nq = seq // t
    step = pl.program_id(0) * pl.num_programs(1) + pl.program_id(1)

    def chunk_rows(a, chunk):
        rows = cast_in[a].shape[1]
        return pl.ds(pl.multiple_of(chunk * rows, 16), rows)

    def cast_loads(chunk, slot):
        return [pltpu.make_async_copy(cast_src[a].at[chunk_rows(a, chunk), :], cast_in[a].at[slot], sem_in.at[slot, a])
                for a in range(n_cast)]

    def cast_stores(chunk, slot):
        return [pltpu.make_async_copy(cast_out[a].at[slot], cast_dst[a].at[chunk_rows(a, chunk), :], sem_out.at[slot, a])
                for a in range(n_cast)]

    row = lax.broadcasted_iota(jnp.int32, (t, t), 0)
    col = lax.broadcasted_iota(jnp.int32, (t, t), 1)
    later = (row > col).astype(BF16)
    later_m = later[:META_KEYS, :META_KEYS]
    causal = col < row
    meta_ok = lax.broadcasted_iota(jnp.int32, (t, META_KEYS), 1) < N_META

    heads = [slice(hh * HEAD_DIM, (hh + 1) * HEAD_DIM) for hh in range(ATTN_HEADS)]

    def scores(qts, kts, later_mat, mask):
        ys = [lax.dot_general(qt, kt, (((1,), (1,)), ((), ())), preferred_element_type=F32)
              for qt, kt in zip(qts, kts)]
        sums, parts, own = [], [], []
        for y in ys:
            e = jnp.exp2(jnp.minimum(y, -y))
            ls = jnp.minimum(y, 0.0) - jnp.log2(1.0 + e)
            if mask is not None:
                ls = jnp.where(mask, ls, 0.0)
            sums.append(jnp.sum(ls, axis=-1, keepdims=True))
            parts.append(ls.astype(BF16))
            own.append(ls - y)
        logws = []
        for part, o in zip(parts, own):
            logw = o + _dot(part, later_mat)
            if mask is not None:
                logw = jnp.where(mask, logw, -jnp.inf)
            logws.append(logw)
        return sums, logws

    def weighted_values(logws, runs, vts):
        return [_dot(jnp.exp2(logw + run).astype(BF16), vt) for logw, run, vt in zip(logws, runs, vts)]

    def q_loop(i, carry):
        chunk = step * nq + i
        slot = i % 2

        @pl.when(i == 0)
        def _():
            for dma in cast_loads(chunk, slot):
                dma.start()

        @pl.when(i + 1 < nq)
        def _():
            for dma in cast_loads(chunk + 1, 1 - slot):
                dma.start()

        q0 = pl.multiple_of(i * t, t)
        qts = [q_ref[pl.ds(q0, t), sl] for sl in heads]

        def kv(k0):
            return [k_ref[pl.ds(k0, t), sl] for sl in heads], [v_ref[pl.ds(k0, t), sl] for sl in heads]

        kts, vts = kv(q0)
        runs, logws = scores(qts, kts, later, causal)
        _, logws_meta = scores(qts, [km_ref[:, sl] for sl in heads], later_m, meta_ok)
        accs = weighted_values(logws, [0.0] * ATTN_HEADS, vts)

        def kv_loop(jj, c):
            runs, accs = c
            kts, vts = kv(pl.multiple_of((i - 1 - jj) * t, t))
            sums, logws = scores(qts, kts, later, None)
            pvs = weighted_values(logws, runs, vts)
            return [run + s for run, s in zip(runs, sums)], [acc + pv for acc, pv in zip(accs, pvs)]

        runs, accs = lax.fori_loop(0, i, kv_loop, (runs, accs))
        pvs = weighted_values(logws_meta, runs, [vm_ref[:, sl] for sl in heads])
        for sl, acc, pv in zip(heads, accs, pvs):
            o_ref[pl.ds(q0, t), sl] = (acc + pv).astype(o_ref.dtype)

        for dma in cast_loads(chunk, slot):
            dma.wait()

        @pl.when(i >= 2)
        def _():
            for dma in cast_stores(chunk - 2, slot):
                dma.wait()

        for a in range(n_cast):
            cast_out[a][slot] = cast_in[a][slot].astype(BF16)
        for dma in cast_stores(chunk, slot):
            dma.start()
        return carry

    lax.fori_loop(0, nq, q_loop, 0)
    for back in (2, 1):
        for dma in cast_stores(step * nq + nq - back, (nq - back) % 2):
            dma.wait()


def attention(qkv, kv_meta, cast_srcs, *, batch, seq):
    w = ATTN_HEADS * HEAD_DIM
    nb = D_ATTN // w
    n_cast = len(cast_srcs)
    n_chunks = batch * nb * (seq // ATTN_TILE)
    chunk_shapes = [(a.shape[0] // n_chunks, a.shape[1]) for a in cast_srcs]
    any_spec = pl.BlockSpec(memory_space=pl.ANY)
    outs = pl.pallas_call(
        functools.partial(_attn_kernel, seq=seq, n_cast=n_cast),
        grid=(batch, nb),
        in_specs=[
            pl.BlockSpec((seq, w), lambda b, hp: (b, 2 * nb + hp)),
            pl.BlockSpec((seq, w), lambda b, hp: (b, hp)),
            pl.BlockSpec((seq, w), lambda b, hp: (b, nb + hp)),
            pl.BlockSpec((META_KEYS, w), lambda b, hp: (0, hp)),
            pl.BlockSpec((META_KEYS, w), lambda b, hp: (0, nb + hp)),
        ] + [any_spec] * n_cast,
        out_specs=[pl.BlockSpec((seq, w), lambda b, hp: (b, hp))] + [any_spec] * n_cast,
        out_shape=[jax.ShapeDtypeStruct((batch * seq, D_ATTN), BF16)]
        + [jax.ShapeDtypeStruct(a.shape, BF16) for a in cast_srcs],
        scratch_shapes=[pltpu.VMEM((2,) + cs, F32) for cs in chunk_shapes]
        + [pltpu.VMEM((2,) + cs, BF16) for cs in chunk_shapes]
        + [pltpu.SemaphoreType.DMA((2, n_cast)), pltpu.SemaphoreType.DMA((2, n_cast))],
        compiler_params=_params("parallel", "parallel"),
        name="attention",
    )(qkv, qkv, qkv, kv_meta, kv_meta, *cast_srcs)
    return outs[0], outs[1:]


def _oproj_router_kernel(o_ref, wo_ref, h_ref, g_ref, wr_ref, h3_ref, hn_ref, idx_ref, gate_ref):
    h3 = h_ref[...] + _dot(o_ref[...], wo_ref[...])
    h3_ref[...] = h3
    hn = _rms(h3, g_ref[...])
    hn_ref[...] = hn
    hn_hi, hn_lo = _split_bf16(hn)
    wr_hi, wr_lo = _split_bf16(wr_ref[...])
    logits = _dot(hn_hi, wr_hi) + (_dot(hn_hi, wr_lo) + _dot(hn_lo, wr_hi))
    e = lax.broadcasted_iota(jnp.int32, logits.shape, 1)
    m1 = jnp.max(logits, axis=-1, keepdims=True)
    i1 = jnp.min(jnp.where(logits == m1, e, N_EXPERTS), axis=-1, keepdims=True)
    rest = jnp.where(e == i1, -jnp.inf, logits)
    m2 = jnp.max(rest, axis=-1, keepdims=True)
    i2 = jnp.min(jnp.where(rest == m2, e, N_EXPERTS), axis=-1, keepdims=True)
    p = jnp.exp(m2 - m1)
    g1 = 1.0 / (1.0 + p)
    k = lax.broadcasted_iota(jnp.int32, idx_ref.shape, 1)
    idx_ref[...] = jnp.where(k == 0, i1, i2)
    gate_ref[...] = jnp.where(k == 0, g1, p * g1)


def oproj_router(o, wo, h, g, w_router, *, tm):
    m, d = h.shape
    return pl.pallas_call(
        _oproj_router_kernel,
        grid=(m // tm,),
        in_specs=[
            pl.BlockSpec((tm, d), lambda i: (i, 0)),
            _resident((d, d), lambda i: (0, 0)),
            pl.BlockSpec((tm, d), lambda i: (i, 0)),
            _resident((1, d), lambda i: (0, 0)),
            _resident((d, N_EXPERTS), lambda i: (0, 0)),
        ],
        out_specs=[
            pl.BlockSpec((tm, d), lambda i: (i, 0)),
            pl.BlockSpec((tm, d), lambda i: (i, 0)),
            pl.BlockSpec((tm, TOP_K), lambda i: (i, 0)),
            pl.BlockSpec((tm, TOP_K), lambda i: (i, 0)),
        ],
        out_shape=[
            jax.ShapeDtypeStruct((m, d), F32),
            jax.ShapeDtypeStruct((m, d), F32),
            jax.ShapeDtypeStruct((m, TOP_K), jnp.int32),
            jax.ShapeDtypeStruct((m, TOP_K), F32),
        ],
        compiler_params=_params("parallel"),
        name="oproj_router",
    )(o, wo, h, g, w_router)


ROW_DMA_UNROLL = 8
MOE_TILE = 1024
MOE_TILE_PARTS = 4


def _dispatch_kernel(dest_ref, gend_ref, nu_ref, hn_ref, x_hbm, zero_ref, sem, zsem, *, tm, tile, n_tiles):
    base = pl.program_id(0) * (tm * TOP_K)

    @pl.when(pl.program_id(0) == 0)
    def _():
        zero_ref[...] = jnp.zeros_like(zero_ref)

        def zero_tile(start):
            return pltpu.make_async_copy(zero_ref, x_hbm.at[pl.ds(pl.multiple_of(start, tile), tile), :], zsem)

        def nonempty(e):
            return gend_ref[e] > (gend_ref[e - 1] if e else 0)

        for e in range(N_EXPERTS):
            @pl.when(nonempty(e))
            def _():
                zero_tile(gend_ref[e] - tile).start()

        for e in range(N_EXPERTS):
            @pl.when(nonempty(e))
            def _():
                zero_tile(0).wait()

        def tail_start(j, c):
            zero_tile(j * tile).start()
            return c

        def tail_wait(j, c):
            zero_tile(0).wait()
            return c

        lax.fori_loop(nu_ref[0], n_tiles, tail_start, 0)
        lax.fori_loop(nu_ref[0], n_tiles, tail_wait, 0)

    def row(r, k, dst):
        return pltpu.make_async_copy(hn_ref.at[pl.ds(r, 1), :], x_hbm.at[pl.ds(dst, 1), :], sem)

    def issue(r, c):
        for k in range(TOP_K):
            row(r, k, dest_ref[base + TOP_K * r + k]).start(priority=k)
        return c

    lax.fori_loop(0, tm, issue, 0, unroll=ROW_DMA_UNROLL)

    def wait(r, c):
        for k in range(TOP_K):
            row(r, k, 0).wait()
        return c

    lax.fori_loop(0, tm, wait, 0, unroll=ROW_DMA_UNROLL)


def moe_dispatch(hn, dest, gend, n_used, padded_rows, *, tm, tile):
    t, d = hn.shape
    grid_spec = pltpu.PrefetchScalarGridSpec(
        num_scalar_prefetch=3,
        grid=(t // tm,),
        in_specs=[pl.BlockSpec((tm, d), lambda i, *_: (i, 0))],
        out_specs=pl.BlockSpec(memory_space=pl.ANY),
        scratch_shapes=[pltpu.VMEM((tile, d), F32), pltpu.SemaphoreType.DMA(()), pltpu.SemaphoreType.DMA(())],
    )
    return pl.pallas_call(
        functools.partial(_dispatch_kernel, tm=tm, tile=tile, n_tiles=padded_rows // tile),
        grid_spec=grid_spec,
        out_shape=jax.ShapeDtypeStruct((padded_rows, d), F32),
        compiler_params=_params("arbitrary"),
        name="moe_dispatch",
    )(dest.reshape(-1), gend, n_used, hn)


def _moe_kernel(te_ref, nu_ref, nv_ref, x_ref, wg_ref, wu_ref, wd_ref, o_ref, xb_ref):
    i = pl.program_id(0)
    f = pl.program_id(1)
    part = x_ref.shape[0] // MOE_TILE_PARTS
    n_valid = jnp.where(i < nu_ref[0], nv_ref[i], 0)

    @pl.when(f == 0)
    def _():
        xb_ref[...] = x_ref[...].astype(BF16)
        o_ref[...] = jnp.zeros_like(o_ref)

    for parts in range(1, MOE_TILE_PARTS + 1):
        rows = parts * part

        @pl.when((n_valid > rows - part) & (n_valid <= rows))
        def _():
            o_ref[0:rows, :] += _swiglu_partial(xb_ref[0:rows, :], wg_ref, wu_ref, wd_ref)


def moe_experts(x, tile_e, n_used, n_valid, w_gate, w_up, w_down, *, tm, tf):
    p, d = x.shape
    nf = w_gate.shape[2] // tf

    def wmap_in(i, f, te, nu, nv):
        return (te[i], 0, jnp.where(i < nu[0], f, nf - 1))

    def wmap_down(i, f, te, nu, nv):
        return (te[i], jnp.where(i < nu[0], f, nf - 1), 0)

    grid_spec = pltpu.PrefetchScalarGridSpec(
        num_scalar_prefetch=3,
        grid=(p // tm, nf),
        in_specs=[
            pl.BlockSpec((tm, d), lambda i, f, *_: (i, 0)),
            pl.BlockSpec((None, d, tf), wmap_in),
            pl.BlockSpec((None, d, tf), wmap_in),
            pl.BlockSpec((None, tf, d), wmap_down),
        ],
        out_specs=pl.BlockSpec((tm, d), lambda i, f, *_: (i, 0)),
        scratch_shapes=[pltpu.VMEM((tm, d), BF16)],
    )
    return pl.pallas_call(
        _moe_kernel,
        grid_spec=grid_spec,
        out_shape=jax.ShapeDtypeStruct((p, d), F32),
        compiler_params=_params("arbitrary", "arbitrary"),
        name="moe_experts",
    )(tile_e, n_used, n_valid, x, w_gate, w_up, w_down)


def _combine_kernel(dest_ref, h_ref, gate_ref, g_ref, y_hbm, o_ref, ybuf, sem, *, tm, n_steps):
    i = pl.program_id(0)

    def row(slot, r, k, src):
        return pltpu.make_async_copy(y_hbm.at[pl.ds(src, 1), :], ybuf.at[slot, k, pl.ds(r, 1), :], sem.at[slot])

    def fetch(step, slot):
        base = step * (tm * TOP_K)

        def issue(r, c):
            for k in range(TOP_K):
                row(slot, r, k, dest_ref[base + TOP_K * r + k]).start(priority=k)
            return c

        lax.fori_loop(0, tm, issue, 0, unroll=ROW_DMA_UNROLL)

    @pl.when(i == 0)
    def _():
        fetch(0, 0)

    @pl.when(i + 1 < n_steps)
    def _():
        fetch(i + 1, (i + 1) % 2)

    slot = i % 2

    def wait(r, c):
        for k in range(TOP_K):
            row(slot, r, k, 0).wait()
        return c

    lax.fori_loop(0, tm, wait, 0, unroll=ROW_DMA_UNROLL)
    gate = gate_ref[...]
    y = h_ref[...] + gate[:, 0:1] * ybuf[slot, 0] + gate[:, 1:2] * ybuf[slot, 1]
    o_ref[...] = _rms(y, g_ref[...])


def moe_combine_norm(h, gates, y, dest, g, *, tm):
    t, d = h.shape
    n_steps = t // tm
    grid_spec = pltpu.PrefetchScalarGridSpec(
        num_scalar_prefetch=1,
        grid=(n_steps,),
        in_specs=[
            pl.BlockSpec((tm, d), lambda i, dest_ref: (i, 0)),
            pl.BlockSpec((tm, TOP_K), lambda i, dest_ref: (i, 0)),
            pl.BlockSpec((1, d), lambda i, dest_ref: (0, 0)),
            pl.BlockSpec(memory_space=pl.ANY),
        ],
        out_specs=pl.BlockSpec((tm, d), lambda i, dest_ref: (i, 0)),
        scratch_shapes=[pltpu.VMEM((2, TOP_K, tm, d), F32), pltpu.SemaphoreType.DMA((2,))],
    )
    return pl.pallas_call(
        functools.partial(_combine_kernel, tm=tm, n_steps=n_steps),
        grid_spec=grid_spec,
        out_shape=jax.ShapeDtypeStruct((t, d), F32),
        compiler_params=_params("arbitrary"),
        name="moe_combine_norm",
    )(dest.reshape(-1), h, gates, g, y)


def routing_tables(idx, *, tm):
    t = idx.shape[0]
    rows = TOP_K * t
    n_tiles = rows // tm + N_EXPERTS
    e_flat = idx.reshape(rows)
    onehot = (e_flat[:, None] == jnp.arange(N_EXPERTS, dtype=jnp.int32)[None, :]).astype(jnp.int32)
    csum = jnp.cumsum(onehot, axis=0)
    rank = jnp.sum((csum - onehot) * onehot, axis=1)
    counts = csum[-1]
    padded = ((counts + tm - 1) // tm) * tm
    gend = jnp.cumsum(padded)
    gstart = gend - padded
    dest = (jnp.sum(onehot * gstart[None, :], axis=1) + rank).astype(jnp.int32).reshape(t, TOP_K)
    n_used = (gend[-1] // tm).astype(jnp.int32).reshape(1)
    tile_start = jnp.minimum(jnp.arange(n_tiles, dtype=jnp.int32), n_used - 1) * tm
    tile_e = jnp.sum((tile_start[:, None] >= gend[None, :]).astype(jnp.int32), axis=1).astype(jnp.int32)
    n_valid = jnp.clip(jnp.sum((tile_e[:, None] == jnp.arange(N_EXPERTS)[None, :]) * (gstart + counts)[None, :], axis=1)
                       - tile_start, 0, tm).astype(jnp.int32)
    return dest, tile_e, n_used, n_valid, gend.astype(jnp.int32), n_tiles * tm


def kernel(x, meta_tokens, mix_norm_g, ffn_norm_g, conv_pw1_w, conv_pw1_b, conv_dw_w, conv_dw_b,
           conv_ln_g, conv_ln_b, conv_pw2_w, conv_pw2_b, kv_norm_g, w_kv, w_q, w_o, ffn_w_gate,
           ffn_w_up, ffn_w_down, moe_router, moe_w_gate, moe_w_up, moe_w_down, final_norm_g):
    batch, seq, d = x.shape
    t = batch * seq
    row = lambda v: v.reshape(1, -1).astype(F32)

    pw1 = conv_pw1_w[0].astype(BF16)
    pw1_b = row(conv_pw1_b[0])
    dw_w = jnp.pad(conv_dw_w[0], ((0, CONV_HALO - CONV_WIDTH), (0, 0)))
    pw2 = conv_pw2_w[0].astype(BF16)
    w_kvq = jnp.concatenate([w_kv.astype(BF16), w_q[0].astype(BF16)], axis=1)
    g_kvq = jnp.stack([kv_norm_g, mix_norm_g[1]]).astype(F32)
    wo = w_o[0].astype(BF16)
    fg, fu, fd = ffn_w_gate[0].astype(BF16), ffn_w_up[0].astype(BF16), ffn_w_down[0].astype(BF16)
    moe_w = [moe_w_gate[0], moe_w_up[0], moe_w_down[0]]

    def layer0(h, halo0, *, b, l, tm, tl):
        u = rms_glu(h, row(mix_norm_g[0]), pw1, pw1_b, tm=min(2 * tm, b * l), tn=1024)
        h1 = conv_block(u.reshape(b, l, d), halo0, h.reshape(b, l, d), dw_w, row(conv_dw_b[0]),
                        row(conv_ln_g[0]), row(conv_ln_b[0]), pw2, row(conv_pw2_b[0]), tl=tl)
        h2 = ffn_block(h1.reshape(b * l, d), row(ffn_norm_g[0]), fg, fu, fd, tm=min(2 * tm, b * l), tf=512)
        return u, h2

    hm = jnp.pad(meta_tokens.astype(F32), ((0, META_PAD - N_META), (0, 0)))
    um, hm2 = layer0(hm, jnp.zeros((CONV_HALO, d), BF16), b=1, l=META_PAD, tm=META_PAD, tl=META_PAD)
    kv_meta = rms_proj2(hm2, g_kvq, w_kvq, n_cols=2 * D_ATTN, n_first_cols=2 * D_ATTN, second_scale=1.0,
                        tm=META_PAD, tn=1024)
    kv_meta = jnp.pad(kv_meta[:N_META], ((0, META_KEYS - N_META), (0, 0)))

    halo0 = jnp.concatenate([jnp.zeros((CONV_HALO - N_META, d), BF16), um[:N_META]], axis=0)
    _, h2 = layer0(x.reshape(t, d), halo0, b=batch, l=seq, tm=512, tl=512)

    qkv = rms_proj2(h2, g_kvq, w_kvq, n_cols=3 * D_ATTN, n_first_cols=2 * D_ATTN, second_scale=Q_PRESCALE,
                    tm=1024, tn=1024)
    o, moe_bf16 = attention(qkv, kv_meta, [w.reshape(-1, w.shape[-1]) for w in moe_w], batch=batch, seq=seq)
    mg, mu, md = [wb.reshape(w.shape) for wb, w in zip(moe_bf16, moe_w)]
    h3, hn3, idx, gates = oproj_router(o, wo, h2, row(ffn_norm_g[1]), moe_router[0].astype(F32), tm=512)

    dest, tile_e, n_used, n_valid, gend, padded_rows = routing_tables(idx, tm=MOE_TILE)
    xs = moe_dispatch(hn3, dest, gend, n_used, padded_rows, tm=512, tile=MOE_TILE)
    ys = moe_experts(xs, tile_e, n_used, n_valid, mg, mu, md, tm=MOE_TILE, tf=512)
    out = moe_combine_norm(h3, gates, ys, dest, row(final_norm_g), tm=256)
    return out.reshape(batch, seq, d)
```

```python
import functools
import math

import jax
import jax.numpy as jnp
from jax import lax
from jax.experimental import pallas as pl
from jax.experimental.pallas import tpu as pltpu

D_MODEL = 2048
N_META = 16
META_PAD = 32
N_HEADS = 16
HEAD_DIM = 128
D_ATTN = N_HEADS * HEAD_DIM
CONV_WIDTH = 31
CONV_HALO = 32
D_FF = 5632
N_EXPERTS = 8
TOP_K = 2
RMS_EPS = 1e-6
LN_EPS = 1e-5

VMEM_LIMIT_BYTES = 56 * 1024 * 1024
F32 = jnp.float32
BF16 = jnp.bfloat16
LOG2_E = math.log2(math.e)


def _dot(a, b):
    return jnp.dot(a, b, preferred_element_type=F32)


def _params(*sem):
    return pltpu.CompilerParams(dimension_semantics=sem, vmem_limit_bytes=VMEM_LIMIT_BYTES)


def _resident(shape, index_map):
    return pl.BlockSpec(shape, index_map, pipeline_mode=pl.Buffered(1))


def _rms(h, g):
    var = jnp.mean(h * h, axis=-1, keepdims=True)
    return h * lax.rsqrt(var + RMS_EPS) * g


def _sigmoid(x):
    return 1.0 / (1.0 + jnp.exp(-x))


def _split_bf16(x):
    hi = x.astype(BF16)
    lo = (x - hi.astype(F32)).astype(BF16)
    return hi, lo


def _rms_glu_kernel(h_ref, g_ref, wa_ref, wg_ref, ba_ref, bg_ref, o_ref, hn_ref):
    @pl.when(pl.program_id(1) == 0)
    def _():
        hn_ref[...] = _rms(h_ref[...], g_ref[...]).astype(BF16)

    hn = hn_ref[...]
    half = o_ref.shape[1] // 2
    for sl in (slice(0, half), slice(half, 2 * half)):
        a = _dot(hn, wa_ref[:, sl]) + ba_ref[:, sl]
        g = _dot(hn, wg_ref[:, sl]) + bg_ref[:, sl]
        o_ref[:, sl] = (a * _sigmoid(g)).astype(o_ref.dtype)


def rms_glu(h, g, w, b, *, tm, tn):
    m, d = h.shape
    nb = d // tn
    return pl.pallas_call(
        _rms_glu_kernel,
        grid=(m // tm, nb),
        in_specs=[
            pl.BlockSpec((tm, d), lambda i, n: (i, 0)),
            pl.BlockSpec((1, d), lambda i, n: (0, 0)),
            pl.BlockSpec((d, tn), lambda i, n: (0, n)),
            pl.BlockSpec((d, tn), lambda i, n: (0, n + nb)),
            pl.BlockSpec((1, tn), lambda i, n: (0, n)),
            pl.BlockSpec((1, tn), lambda i, n: (0, n + nb)),
        ],
        out_specs=pl.BlockSpec((tm, tn), lambda i, n: (i, n)),
        out_shape=jax.ShapeDtypeStruct((m, d), BF16),
        scratch_shapes=[pltpu.VMEM((tm, d), BF16)],
        compiler_params=_params("parallel", "arbitrary"),
        name="rms_glu",
    )(h, g, w, w, b, b)


def _rms_proj2_kernel(h_ref, g_ref, w_ref, o_ref, hn_ref, *, n_first, second_scale):
    n = pl.program_id(1)

    @pl.when(n == 0)
    def _():
        h = h_ref[...]
        y = h * lax.rsqrt(jnp.mean(h * h, axis=-1, keepdims=True) + RMS_EPS)
        hn_ref[0] = (y * g_ref[0:1, :]).astype(BF16)
        hn_ref[1] = (y * g_ref[1:2, :]).astype(BF16)

    second = n >= n_first
    r = _dot(hn_ref[second.astype(jnp.int32)], w_ref[...])
    o_ref[...] = (r * jnp.where(second, second_scale, 1.0)).astype(o_ref.dtype)


def rms_proj2(h, g2, w, *, n_cols, n_first_cols, second_scale, tm, tn):
    m, d = h.shape
    return pl.pallas_call(
        functools.partial(_rms_proj2_kernel, n_first=n_first_cols // tn, second_scale=second_scale),
        grid=(m // tm, n_cols // tn),
        in_specs=[
            pl.BlockSpec((tm, d), lambda i, j: (i, 0)),
            pl.BlockSpec((2, d), lambda i, j: (0, 0)),
            pl.BlockSpec((d, tn), lambda i, j: (0, j)),
        ],
        out_specs=pl.BlockSpec((tm, tn), lambda i, j: (i, j)),
        out_shape=jax.ShapeDtypeStruct((m, n_cols), BF16),
        scratch_shapes=[pltpu.VMEM((2, tm, d), BF16)],
        compiler_params=_params("parallel", "arbitrary"),
        name="rms_proj2",
    )(h, g2, w)


CONV_LANES = 256
CONV_ROWS = 64


def _conv_kernel(u_ref, uh_ref, halo0_ref, h_ref, dw_ref, dwb_ref, lng_ref, lnb_ref,
                 w2_ref, b2_ref, o_ref, xp_ref, sh_ref, cv_ref, *, tl):
    i = pl.program_id(1)
    d = u_ref.shape[-1]

    @pl.when(i == 0)
    def _():
        xp_ref[0:CONV_HALO, :] = halo0_ref[...].astype(F32)

    @pl.when(i > 0)
    def _():
        xp_ref[0:CONV_HALO, :] = uh_ref[...].astype(F32)

    xp_ref[CONV_HALO:CONV_HALO + tl, :] = u_ref[...].astype(F32)

    n_sh = tl + CONV_HALO - 8
    rows = min(CONV_ROWS, tl)

    def lane_block(cb, carry):
        c0 = pl.multiple_of(cb * CONV_LANES, CONV_LANES)
        for r in range(1, 8):
            sh_ref[r, 0:n_sh, :] = xp_ref[pl.ds(r, n_sh), pl.ds(c0, CONV_LANES)]

        def row_chunk(rc, carry2):
            r0 = pl.multiple_of(rc * rows, rows)
            acc = jnp.zeros((rows, CONV_LANES), F32)
            for w in range(CONV_WIDTH):
                a, r = divmod(CONV_HALO - (CONV_WIDTH - 1) + w, 8)
                if r == 0:
                    xs = xp_ref[pl.ds(r0 + 8 * a, rows), pl.ds(c0, CONV_LANES)]
                else:
                    xs = sh_ref[r, pl.ds(r0 + 8 * a, rows), :]
                acc = acc + xs * dw_ref[pl.ds(w, 1), pl.ds(c0, CONV_LANES)]
            cv_ref[pl.ds(r0, rows), pl.ds(c0, CONV_LANES)] = acc + dwb_ref[:, pl.ds(c0, CONV_LANES)]
            return carry2

        lax.fori_loop(0, tl // rows, row_chunk, 0)
        return carry

    lax.fori_loop(0, d // CONV_LANES, lane_block, 0)

    half = max(tl // 2, 16)
    for r0 in range(0, tl, half):
        c = cv_ref[r0:r0 + half, :]
        mu = jnp.mean(c, axis=-1, keepdims=True)
        cc = c - mu
        var = jnp.mean(cc * cc, axis=-1, keepdims=True)
        y = cc * lax.rsqrt(var + LN_EPS) * lng_ref[...] + lnb_ref[...]
        s = (y * _sigmoid(y)).astype(BF16)
        o_ref[r0:r0 + half, :] = h_ref[r0:r0 + half, :] + _dot(s, w2_ref[...]) + b2_ref[...]


def conv_block(u, halo0, h, dw_w, dw_b, ln_g, ln_b, w2, b2, *, tl):
    b, l, d = u.shape
    hb = tl // CONV_HALO
    return pl.pallas_call(
        functools.partial(_conv_kernel, tl=tl),
        grid=(b, l // tl),
        in_specs=[
            pl.BlockSpec((None, tl, d), lambda bi, i: (bi, i, 0)),
            pl.BlockSpec((None, CONV_HALO, d), lambda bi, i: (bi, jnp.maximum(i * hb - 1, 0), 0)),
            _resident((CONV_HALO, d), lambda bi, i: (0, 0)),
            pl.BlockSpec((None, tl, d), lambda bi, i: (bi, i, 0)),
            _resident((CONV_HALO, d), lambda bi, i: (0, 0)),
            _resident((1, d), lambda bi, i: (0, 0)),
            _resident((1, d), lambda bi, i: (0, 0)),
            _resident((1, d), lambda bi, i: (0, 0)),
            _resident((d, d), lambda bi, i: (0, 0)),
            _resident((1, d), lambda bi, i: (0, 0)),
        ],
        out_specs=pl.BlockSpec((None, tl, d), lambda bi, i: (bi, i, 0)),
        out_shape=jax.ShapeDtypeStruct((b, l, d), F32),
        scratch_shapes=[
            pltpu.VMEM((tl + CONV_HALO, d), F32),
            pltpu.VMEM((8, tl + CONV_HALO, CONV_LANES), F32),
            pltpu.VMEM((tl, d), F32),
        ],
        compiler_params=_params("parallel", "arbitrary"),
        name="conv_block",
    )(u, u, halo0, h, dw_w, dw_b, ln_g, ln_b, w2, b2)


def _swiglu_partial(x, wg_ref, wu_ref, wd_ref):
    half = wg_ref.shape[1] // 2
    acts = []
    for sl in (slice(0, half), slice(half, 2 * half)):
        a = _dot(x, wg_ref[:, sl])
        u = _dot(x, wu_ref[:, sl])
        acts.append((a * _sigmoid(a) * u).astype(BF16))
    return _dot(jnp.concatenate(acts, axis=1), wd_ref[...])


def _ffn_kernel(h_ref, g_ref, wg_ref, wu_ref, wd_ref, o_ref, hn_ref):
    @pl.when(pl.program_id(1) == 0)
    def _():
        h = h_ref[...]
        hn_ref[...] = _rms(h, g_ref[...]).astype(BF16)
        o_ref[...] = h

    o_ref[...] += _swiglu_partial(hn_ref[...], wg_ref, wu_ref, wd_ref)


def ffn_block(h, g, w_gate, w_up, w_down, *, tm, tf):
    m, d = h.shape
    ff = w_gate.shape[1]
    return pl.pallas_call(
        _ffn_kernel,
        grid=(m // tm, ff // tf),
        in_specs=[
            pl.BlockSpec((tm, d), lambda i, f: (i, 0)),
            pl.BlockSpec((1, d), lambda i, f: (0, 0)),
            pl.BlockSpec((d, tf), lambda i, f: (0, f)),
            pl.BlockSpec((d, tf), lambda i, f: (0, f)),
            pl.BlockSpec((tf, d), lambda i, f: (f, 0)),
        ],
        out_specs=pl.BlockSpec((tm, d), lambda i, f: (i, 0)),
        out_shape=jax.ShapeDtypeStruct((m, d), F32),
        scratch_shapes=[pltpu.VMEM((tm, d), BF16)],
        compiler_params=_params("parallel", "arbitrary"),
        name="ffn_block",
    )(h, g, w_gate, w_up, w_down)


ATTN_TILE = 256
ATTN_HEADS = 4
META_KEYS = 128
Q_PRESCALE = -(HEAD_DIM ** -0.5) * LOG2_E


def _attn_kernel(q_ref, k_ref, v_ref, km_ref, vm_ref, *rest, seq, n_cast):
    cast_src = rest[:n_cast]
    o_ref = rest[n_cast]
    cast_dst = rest[n_cast + 1:2 * n_cast + 1]
    cast_in = rest[2 * n_cast + 1:3 * n_cast + 1]
    cast_out = rest[3 * n_cast + 1:4 * n_cast + 1]
    sem_in, sem_out = rest[4 * n_cast + 1:]
    t = ATTN_TILE
    nq = seq // t
    step = pl.program_id(0) * pl.num_programs(1) + pl.program_id(1)

    def chunk_rows(a, chunk):
        rows = cast_in[a].shape[1]
        return pl.ds(pl.multiple_of(chunk * rows, 16), rows)

    def cast_loads(chunk, slot):
        return [pltpu.make_async_copy(cast_src[a].at[chunk_rows(a, chunk), :], cast_in[a].at[slot], sem_in.at[slot, a])
                for a in range(n_cast)]

    def cast_stores(chunk, slot):
        return [pltpu.make_async_copy(cast_out[a].at[slot], cast_dst[a].at[chunk_rows(a, chunk), :], sem_out.at[slot, a])
                for a in range(n_cast)]

    row = lax.broadcasted_iota(jnp.int32, (t, t), 0)
    col = lax.broadcasted_iota(jnp.int32, (t, t), 1)
    later = (row > col).astype(BF16)
    later_m = later[:META_KEYS, :META_KEYS]
    causal = col < row
    meta_ok = lax.broadcasted_iota(jnp.int32, (t, META_KEYS), 1) < N_META

    heads = [slice(hh * HEAD_DIM, (hh + 1) * HEAD_DIM) for hh in range(ATTN_HEADS)]

    def scores(qts, kts, later_mat, mask):
        ys = [lax.dot_general(qt, kt, (((1,), (1,)), ((), ())), preferred_element_type=F32)
              for qt, kt in zip(qts, kts)]
        sums, parts, own = [], [], []
        for y in ys:
            e = jnp.exp2(jnp.minimum(y, -y))
            ls = jnp.minimum(y, 0.0) - jnp.log2(1.0 + e)
            if mask is not None:
                ls = jnp.where(mask, ls, 0.0)
            sums.append(jnp.sum(ls, axis=-1, keepdims=True))
            parts.append(ls.astype(BF16))
            own.append(ls - y)
        logws = []
        for part, o in zip(parts, own):
            logw = o + _dot(part, later_mat)
            if mask is not None:
                logw = jnp.where(mask, logw, -jnp.inf)
            logws.append(logw)
        return sums, logws

    def weighted_values(logws, runs, vts):
        return [_dot(jnp.exp2(logw + run).astype(BF16), vt) for logw, run, vt in zip(logws, runs, vts)]

    def q_loop(i, carry):
        chunk = step * nq + i
        slot = i % 2

        @pl.when(i == 0)
        def _():
            for dma in cast_loads(chunk, slot):
                dma.start()

        @pl.when(i + 1 < nq)
        def _():
            for dma in cast_loads(chunk + 1, 1 - slot):
                dma.start()

        q0 = pl.multiple_of(i * t, t)
        qts = [q_ref[pl.ds(q0, t), sl] for sl in heads]

        def kv(k0):
            return [k_ref[pl.ds(k0, t), sl] for sl in heads], [v_ref[pl.ds(k0, t), sl] for sl in heads]

        kts, vts = kv(q0)
        runs, logws = scores(qts, kts, later, causal)
        _, logws_meta = scores(qts, [km_ref[:, sl] for sl in heads], later_m, meta_ok)
        accs = weighted_values(logws, [0.0] * ATTN_HEADS, vts)

        def kv_loop(jj, c):
            runs, accs = c
            kts, vts = kv(pl.multiple_of((i - 1 - jj) * t, t))
            sums, logws = scores(qts, kts, later, None)
            pvs = weighted_values(logws, runs, vts)
            return [run + s for run, s in zip(runs, sums)], [acc + pv for acc, pv in zip(accs, pvs)]

        runs, accs = lax.fori_loop(0, i, kv_loop, (runs, accs))
        pvs = weighted_values(logws_meta, runs, [vm_ref[:, sl] for sl in heads])
        for sl, acc, pv in zip(heads, accs, pvs):
            o_ref[pl.ds(q0, t), sl] = (acc + pv).astype(o_ref.dtype)

        for dma in cast_loads(chunk, slot):
            dma.wait()

        @pl.when(i >= 2)
        def _():
            for dma in cast_stores(chunk - 2, slot):
                dma.wait()

        for a in range(n_cast):
            cast_out[a][slot] = cast_in[a][slot].astype(BF16)
        for dma in cast_stores(chunk, slot):
            dma.start()
        return carry

    lax.fori_loop(0, nq, q_loop, 0)
    for back in (2, 1):
        for dma in cast_stores(step * nq + nq - back, (nq - back) % 2):
            dma.wait()


def attention(qkv, kv_meta, cast_srcs, *, batch, seq):
    w = ATTN_HEADS * HEAD_DIM
    nb = D_ATTN // w
    n_cast = len(cast_srcs)
    n_chunks = batch * nb * (seq // ATTN_TILE)
    chunk_shapes = [(a.shape[0] // n_chunks, a.shape[1]) for a in cast_srcs]
    any_spec = pl.BlockSpec(memory_space=pl.ANY)
    outs = pl.pallas_call(
        functools.partial(_attn_kernel, seq=seq, n_cast=n_cast),
        grid=(batch, nb),
        in_specs=[
            pl.BlockSpec((seq, w), lambda b, hp: (b, 2 * nb + hp)),
            pl.BlockSpec((seq, w), lambda b, hp: (b, hp)),
            pl.BlockSpec((seq, w), lambda b, hp: (b, nb + hp)),
            pl.BlockSpec((META_KEYS, w), lambda b, hp: (0, hp)),
            pl.BlockSpec((META_KEYS, w), lambda b, hp: (0, nb + hp)),
        ] + [any_spec] * n_cast,
        out_specs=[pl.BlockSpec((seq, w), lambda b, hp: (b, hp))] + [any_spec] * n_cast,
        out_shape=[jax.ShapeDtypeStruct((batch * seq, D_ATTN), BF16)]
        + [jax.ShapeDtypeStruct(a.shape, BF16) for a in cast_srcs],
        scratch_shapes=[pltpu.VMEM((2,) + cs, F32) for cs in chunk_shapes]
        + [pltpu.VMEM((2,) + cs, BF16) for cs in chunk_shapes]
        + [pltpu.SemaphoreType.DMA((2, n_cast)), pltpu.SemaphoreType.DMA((2, n_cast))],
        compiler_params=_params("parallel", "parallel"),
        name="attention",
    )(qkv, qkv, qkv, kv_meta, kv_meta, *cast_srcs)
    return outs[0], outs[1:]


def _oproj_router_kernel(o_ref, wo_ref, h_ref, g_ref, wr_ref, h3_ref, hn_ref, idx_ref, gate_ref):
    wr_hi, wr_lo = _split_bf16(wr_ref[...])
    half = o_ref.shape[0] // 2
    for rows in (slice(0, half), slice(half, 2 * half)):
        h3 = h_ref[rows, :] + _dot(o_ref[rows, :], wo_ref[...])
        h3_ref[rows, :] = h3
        hn = _rms(h3, g_ref[...])
        hn_ref[rows, :] = hn
        hn_hi, hn_lo = _split_bf16(hn)
        logits = _dot(hn_hi, wr_hi) + (_dot(hn_hi, wr_lo) + _dot(hn_lo, wr_hi))
        e = lax.broadcasted_iota(jnp.int32, logits.shape, 1)
        m1 = jnp.max(logits, axis=-1, keepdims=True)
        i1 = jnp.min(jnp.where(logits == m1, e, N_EXPERTS), axis=-1, keepdims=True)
        rest = jnp.where(e == i1, -jnp.inf, logits)
        m2 = jnp.max(rest, axis=-1, keepdims=True)
        i2 = jnp.min(jnp.where(rest == m2, e, N_EXPERTS), axis=-1, keepdims=True)
        p = jnp.exp(m2 - m1)
        g1 = 1.0 / (1.0 + p)
        k = lax.broadcasted_iota(jnp.int32, (half, TOP_K), 1)
        idx_ref[rows, :] = jnp.where(k == 0, i1, i2)
        gate_ref[rows, :] = jnp.where(k == 0, g1, p * g1)


def oproj_router(o, wo, h, g, w_router, *, tm):
    m, d = h.shape
    return pl.pallas_call(
        _oproj_router_kernel,
        grid=(m // tm,),
        in_specs=[
            pl.BlockSpec((tm, d), lambda i: (i, 0)),
            _resident((d, d), lambda i: (0, 0)),
            pl.BlockSpec((tm, d), lambda i: (i, 0)),
            _resident((1, d), lambda i: (0, 0)),
            _resident((d, N_EXPERTS), lambda i: (0, 0)),
        ],
        out_specs=[
            pl.BlockSpec((tm, d), lambda i: (i, 0)),
            pl.BlockSpec((tm, d), lambda i: (i, 0)),
            pl.BlockSpec((tm, TOP_K), lambda i: (i, 0)),
            pl.BlockSpec((tm, TOP_K), lambda i: (i, 0)),
        ],
        out_shape=[
            jax.ShapeDtypeStruct((m, d), F32),
            jax.ShapeDtypeStruct((m, d), F32),
            jax.ShapeDtypeStruct((m, TOP_K), jnp.int32),
            jax.ShapeDtypeStruct((m, TOP_K), F32),
        ],
        compiler_params=_params("parallel"),
        name="oproj_router",
    )(o, wo, h, g, w_router)


ROW_DMA_UNROLL = 8
SUBLANES = 8
MOE_TILE = 1024
MOE_TILE_PARTS = 4


def _dispatch_kernel(dest_ref, gend_ref, nu_ref, hn_ref, x_hbm, zero_ref, sem, zsem, *, tm, tile, n_tiles):
    base = pl.program_id(0) * (tm * TOP_K)

    @pl.when(pl.program_id(0) == 0)
    def _():
        zero_ref[...] = jnp.zeros_like(zero_ref)

        def zero_tile(start):
            return pltpu.make_async_copy(zero_ref, x_hbm.at[pl.ds(pl.multiple_of(start, tile), tile), :], zsem)

        def nonempty(e):
            return gend_ref[e] > (gend_ref[e - 1] if e else 0)

        for e in range(N_EXPERTS):
            @pl.when(nonempty(e))
            def _():
                zero_tile(gend_ref[e] - tile).start()

        for e in range(N_EXPERTS):
            @pl.when(nonempty(e))
            def _():
                zero_tile(0).wait()

        def tail_start(j, c):
            zero_tile(j * tile).start()
            return c

        def tail_wait(j, c):
            zero_tile(0).wait()
            return c

        lax.fori_loop(nu_ref[0], n_tiles, tail_start, 0)
        lax.fori_loop(nu_ref[0], n_tiles, tail_wait, 0)

    def row(r, k, dst):
        return pltpu.make_async_copy(hn_ref.at[pl.ds(r, 1), :], x_hbm.at[pl.ds(dst, 1), :], sem)

    def issue(r, c):
        for k in range(TOP_K):
            row(r, k, dest_ref[base + TOP_K * r + k]).start(priority=k)
        return c

    lax.fori_loop(0, tm, issue, 0, unroll=ROW_DMA_UNROLL)

    def wait(r, c):
        for k in range(TOP_K):
            row(r, k, 0).wait()
        return c

    lax.fori_loop(0, tm, wait, 0, unroll=ROW_DMA_UNROLL)


def moe_dispatch(hn, dest, gend, n_used, padded_rows, *, tm, tile):
    t, d = hn.shape
    grid_spec = pltpu.PrefetchScalarGridSpec(
        num_scalar_prefetch=3,
        grid=(t // tm,),
        in_specs=[pl.BlockSpec((tm, d), lambda i, *_: (i, 0))],
        out_specs=pl.BlockSpec(memory_space=pl.ANY),
        scratch_shapes=[pltpu.VMEM((tile, d), F32), pltpu.SemaphoreType.DMA(()), pltpu.SemaphoreType.DMA(())],
    )
    return pl.pallas_call(
        functools.partial(_dispatch_kernel, tm=tm, tile=tile, n_tiles=padded_rows // tile),
        grid_spec=grid_spec,
        out_shape=jax.ShapeDtypeStruct((padded_rows, d), F32),
        compiler_params=_params("arbitrary"),
        name="moe_dispatch",
    )(dest.reshape(-1), gend, n_used, hn)


def _moe_kernel(te_ref, nu_ref, nv_ref, x_ref, wg_ref, wu_ref, wd_ref, o_ref, xb_ref):
    i = pl.program_id(0)
    f = pl.program_id(1)
    part = x_ref.shape[0] // MOE_TILE_PARTS
    n_valid = jnp.where(i < nu_ref[0], nv_ref[i], 0)

    @pl.when(f == 0)
    def _():
        xb_ref[...] = x_ref[...].astype(BF16)
        o_ref[...] = jnp.zeros_like(o_ref)

    for parts in range(1, MOE_TILE_PARTS + 1):
        rows = parts * part

        @pl.when((n_valid > rows - part) & (n_valid <= rows))
        def _():
            o_ref[0:rows, :] += _swiglu_partial(xb_ref[0:rows, :], wg_ref, wu_ref, wd_ref)


def moe_experts(x, tile_e, n_used, n_valid, w_gate, w_up, w_down, *, tm, tf):
    p, d = x.shape
    nf = w_gate.shape[2] // tf

    def wmap_in(i, f, te, nu, nv):
        return (te[i], 0, jnp.where(i < nu[0], f, nf - 1))

    def wmap_down(i, f, te, nu, nv):
        return (te[i], jnp.where(i < nu[0], f, nf - 1), 0)

    grid_spec = pltpu.PrefetchScalarGridSpec(
        num_scalar_prefetch=3,
        grid=(p // tm, nf),
        in_specs=[
            pl.BlockSpec((tm, d), lambda i, f, *_: (i, 0)),
            pl.BlockSpec((None, d, tf), wmap_in),
            pl.BlockSpec((None, d, tf), wmap_in),
            pl.BlockSpec((None, tf, d), wmap_down),
        ],
        out_specs=pl.BlockSpec((tm, d), lambda i, f, *_: (i, 0)),
        scratch_shapes=[pltpu.VMEM((tm, d), BF16)],
    )
    return pl.pallas_call(
        _moe_kernel,
        grid_spec=grid_spec,
        out_shape=jax.ShapeDtypeStruct((p, d), F32),
        compiler_params=_params("arbitrary", "arbitrary"),
        name="moe_experts",
    )(tile_e, n_used, n_valid, x, w_gate, w_up, w_down)


def _combine_kernel(dest_ref, h_ref, gate_ref, g_ref, y_hbm, o_ref, ybuf, sem, *, tm, n_steps):
    i = pl.program_id(0)

    def row(slot, grp, sub, k, src):
        return pltpu.make_async_copy(y_hbm.at[pl.ds(src, 1), :], ybuf.at[slot, k, grp, pl.ds(sub, 1), :], sem.at[slot])

    def fetch(step, slot):
        base = step * (tm * TOP_K)

        def issue(grp, c):
            for sub in range(SUBLANES):
                for k in range(TOP_K):
                    src = dest_ref[base + TOP_K * (grp * SUBLANES + sub) + k]
                    row(slot, grp, sub, k, src).start(priority=k)
            return c

        lax.fori_loop(0, tm // SUBLANES, issue, 0)

    @pl.when(i == 0)
    def _():
        fetch(0, 0)

    @pl.when(i + 1 < n_steps)
    def _():
        fetch(i + 1, (i + 1) % 2)

    slot = i % 2

    def wait(grp, c):
        for sub in range(SUBLANES):
            for k in range(TOP_K):
                row(slot, grp, sub, k, 0).wait()
        return c

    lax.fori_loop(0, tm // SUBLANES, wait, 0)
    gate = gate_ref[...]
    d = h_ref.shape[1]
    y = (h_ref[...] + gate[:, 0:1] * ybuf[slot, 0].reshape(tm, d) + gate[:, 1:2] * ybuf[slot, 1].reshape(tm, d))
    o_ref[...] = _rms(y, g_ref[...])


def moe_combine_norm(h, gates, y, dest, g, *, tm):
    t, d = h.shape
    n_steps = t // tm
    grid_spec = pltpu.PrefetchScalarGridSpec(
        num_scalar_prefetch=1,
        grid=(n_steps,),
        in_specs=[
            pl.BlockSpec((tm, d), lambda i, dest_ref: (i, 0)),
            pl.BlockSpec((tm, TOP_K), lambda i, dest_ref: (i, 0)),
            pl.BlockSpec((1, d), lambda i, dest_ref: (0, 0)),
            pl.BlockSpec(memory_space=pl.ANY),
        ],
        out_specs=pl.BlockSpec((tm, d), lambda i, dest_ref: (i, 0)),
        scratch_shapes=[pltpu.VMEM((2, TOP_K, tm // SUBLANES, SUBLANES, d), F32), pltpu.SemaphoreType.DMA((2,))],
    )
    return pl.pallas_call(
        functools.partial(_combine_kernel, tm=tm, n_steps=n_steps),
        grid_spec=grid_spec,
        out_shape=jax.ShapeDtypeStruct((t, d), F32),
        compiler_params=_params("arbitrary"),
        name="moe_combine_norm",
    )(dest.reshape(-1), h, gates, g, y)


def routing_tables(idx, *, tm):
    t = idx.shape[0]
    rows = TOP_K * t
    n_tiles = rows // tm + N_EXPERTS
    e_flat = idx.reshape(rows)
    onehot = (e_flat[:, None] == jnp.arange(N_EXPERTS, dtype=jnp.int32)[None, :]).astype(jnp.int32)
    csum = jnp.cumsum(onehot, axis=0)
    rank = jnp.sum((csum - onehot) * onehot, axis=1)
    counts = csum[-1]
    padded = ((counts + tm - 1) // tm) * tm
    gend = jnp.cumsum(padded)
    gstart = gend - padded
    dest = (jnp.sum(onehot * gstart[None, :], axis=1) + rank).astype(jnp.int32).reshape(t, TOP_K)
    n_used = (gend[-1] // tm).astype(jnp.int32).reshape(1)
    tile_start = jnp.minimum(jnp.arange(n_tiles, dtype=jnp.int32), n_used - 1) * tm
    tile_e = jnp.sum((tile_start[:, None] >= gend[None, :]).astype(jnp.int32), axis=1).astype(jnp.int32)
    n_valid = jnp.clip(jnp.sum((tile_e[:, None] == jnp.arange(N_EXPERTS)[None, :]) * (gstart + counts)[None, :], axis=1)
                       - tile_start, 0, tm).astype(jnp.int32)
    return dest, tile_e, n_used, n_valid, gend.astype(jnp.int32), n_tiles * tm


def kernel(x, meta_tokens, mix_norm_g, ffn_norm_g, conv_pw1_w, conv_pw1_b, conv_dw_w, conv_dw_b,
           conv_ln_g, conv_ln_b, conv_pw2_w, conv_pw2_b, kv_norm_g, w_kv, w_q, w_o, ffn_w_gate,
           ffn_w_up, ffn_w_down, moe_router, moe_w_gate, moe_w_up, moe_w_down, final_norm_g):
    batch, seq, d = x.shape
    t = batch * seq
    row = lambda v: v.reshape(1, -1).astype(F32)

    pw1 = conv_pw1_w[0].astype(BF16)
    pw1_b = row(conv_pw1_b[0])
    dw_w = jnp.pad(conv_dw_w[0], ((0, CONV_HALO - CONV_WIDTH), (0, 0)))
    pw2 = conv_pw2_w[0].astype(BF16)
    w_kvq = jnp.concatenate([w_kv.astype(BF16), w_q[0].astype(BF16)], axis=1)
    g_kvq = jnp.stack([kv_norm_g, mix_norm_g[1]]).astype(F32)
    wo = w_o[0].astype(BF16)
    fg, fu, fd = ffn_w_gate[0].astype(BF16), ffn_w_up[0].astype(BF16), ffn_w_down[0].astype(BF16)
    moe_w = [moe_w_gate[0], moe_w_up[0], moe_w_down[0]]

    def layer0(h, halo0, *, b, l, tm, tl):
        u = rms_glu(h, row(mix_norm_g[0]), pw1, pw1_b, tm=min(2 * tm, b * l), tn=1024)
        h1 = conv_block(u.reshape(b, l, d), halo0, h.reshape(b, l, d), dw_w, row(conv_dw_b[0]),
                        row(conv_ln_g[0]), row(conv_ln_b[0]), pw2, row(conv_pw2_b[0]), tl=tl)
        h2 = ffn_block(h1.reshape(b * l, d), row(ffn_norm_g[0]), fg, fu, fd, tm=min(2 * tm, b * l), tf=512)
        return u, h2

    hm = jnp.pad(meta_tokens.astype(F32), ((0, META_PAD - N_META), (0, 0)))
    um, hm2 = layer0(hm, jnp.zeros((CONV_HALO, d), BF16), b=1, l=META_PAD, tm=META_PAD, tl=META_PAD)
    kv_meta = rms_proj2(hm2, g_kvq, w_kvq, n_cols=2 * D_ATTN, n_first_cols=2 * D_ATTN, second_scale=1.0,
                        tm=META_PAD, tn=1024)
    kv_meta = jnp.pad(kv_meta[:N_META], ((0, META_KEYS - N_META), (0, 0)))

    halo0 = jnp.concatenate([jnp.zeros((CONV_HALO - N_META, d), BF16), um[:N_META]], axis=0)
    _, h2 = layer0(x.reshape(t, d), halo0, b=batch, l=seq, tm=512, tl=512)

    qkv = rms_proj2(h2, g_kvq, w_kvq, n_cols=3 * D_ATTN, n_first_cols=2 * D_ATTN, second_scale=Q_PRESCALE,
                    tm=1024, tn=1024)
    o, moe_bf16 = attention(qkv, kv_meta, [w.reshape(-1, w.shape[-1]) for w in moe_w], batch=batch, seq=seq)
    mg, mu, md = [wb.reshape(w.shape) for wb, w in zip(moe_bf16, moe_w)]
    h3, hn3, idx, gates = oproj_router(o, wo, h2, row(ffn_norm_g[1]), moe_router[0].astype(F32), tm=512)

    dest, tile_e, n_used, n_valid, gend, padded_rows = routing_tables(idx, tm=MOE_TILE)
    xs = moe_dispatch(hn3, dest, gend, n_used, padded_rows, tm=512, tile=MOE_TILE)
    ys = moe_experts(xs, tile_e, n_used, n_valid, mg, mu, md, tm=MOE_TILE, tf=512)
    out = moe_combine_norm(h3, gates, ys, dest, row(final_norm_g), tm=256)
    return out.reshape(batch, seq, d)
```

```python
import functools
import math

import jax
import jax.numpy as jnp
from jax import lax
from jax.experimental import pallas as pl
from jax.experimental.pallas import tpu as pltpu

D_MODEL = 2048
N_META = 16
META_PAD = 32
N_HEADS = 16
HEAD_DIM = 128
D_ATTN = N_HEADS * HEAD_DIM
CONV_WIDTH = 31
CONV_HALO = 32
D_FF = 5632
N_EXPERTS = 8
TOP_K = 2
RMS_EPS = 1e-6
LN_EPS = 1e-5

VMEM_LIMIT_BYTES = 56 * 1024 * 1024
F32 = jnp.float32
BF16 = jnp.bfloat16
LOG2_E = math.log2(math.e)


def _dot(a, b):
    return jnp.dot(a, b, preferred_element_type=F32)


def _params(*sem):
    return pltpu.CompilerParams(dimension_semantics=sem, vmem_limit_bytes=VMEM_LIMIT_BYTES)


def _resident(shape, index_map):
    return pl.BlockSpec(shape, index_map, pipeline_mode=pl.Buffered(1))


def _rms(h, g):
    var = jnp.mean(h * h, axis=-1, keepdims=True)
    return h * lax.rsqrt(var + RMS_EPS) * g


def _sigmoid(x):
    return 1.0 / (1.0 + jnp.exp(-x))


def _split_bf16(x):
    hi = x.astype(BF16)
    lo = (x - hi.astype(F32)).astype(BF16)
    return hi, lo


def _rms_glu_kernel(h_ref, g_ref, wa_ref, wg_ref, ba_ref, bg_ref, o_ref, hn_ref):
    @pl.when(pl.program_id(1) == 0)
    def _():
        hn_ref[...] = _rms(h_ref[...], g_ref[...]).astype(BF16)

    hn = hn_ref[...]
    half = o_ref.shape[1] // 2
    for sl in (slice(0, half), slice(half, 2 * half)):
        a = _dot(hn, wa_ref[:, sl]) + ba_ref[:, sl]
        g = _dot(hn, wg_ref[:, sl]) + bg_ref[:, sl]
        o_ref[:, sl] = (a * _sigmoid(g)).astype(o_ref.dtype)


def rms_glu(h, g, w, b, *, tm, tn):
    m, d = h.shape
    nb = d // tn
    return pl.pallas_call(
        _rms_glu_kernel,
        grid=(m // tm, nb),
        in_specs=[
            pl.BlockSpec((tm, d), lambda i, n: (i, 0)),
            pl.BlockSpec((1, d), lambda i, n: (0, 0)),
            pl.BlockSpec((d, tn), lambda i, n: (0, n)),
            pl.BlockSpec((d, tn), lambda i, n: (0, n + nb)),
            pl.BlockSpec((1, tn), lambda i, n: (0, n)),
            pl.BlockSpec((1, tn), lambda i, n: (0, n + nb)),
        ],
        out_specs=pl.BlockSpec((tm, tn), lambda i, n: (i, n)),
        out_shape=jax.ShapeDtypeStruct((m, d), BF16),
        scratch_shapes=[pltpu.VMEM((tm, d), BF16)],
        compiler_params=_params("parallel", "arbitrary"),
        name="rms_glu",
    )(h, g, w, w, b, b)


def _rms_proj2_kernel(h_ref, g_ref, w_ref, o_ref, hn_ref, *, n_first, second_scale):
    n = pl.program_id(1)

    @pl.when(n == 0)
    def _():
        h = h_ref[...]
        y = h * lax.rsqrt(jnp.mean(h * h, axis=-1, keepdims=True) + RMS_EPS)
        hn_ref[0] = (y * g_ref[0:1, :]).astype(BF16)
        hn_ref[1] = (y * g_ref[1:2, :]).astype(BF16)

    second = n >= n_first
    r = _dot(hn_ref[second.astype(jnp.int32)], w_ref[...])
    o_ref[...] = (r * jnp.where(second, second_scale, 1.0)).astype(o_ref.dtype)


def rms_proj2(h, g2, w, *, n_cols, n_first_cols, second_scale, tm, tn):
    m, d = h.shape
    return pl.pallas_call(
        functools.partial(_rms_proj2_kernel, n_first=n_first_cols // tn, second_scale=second_scale),
        grid=(m // tm, n_cols // tn),
        in_specs=[
            pl.BlockSpec((tm, d), lambda i, j: (i, 0)),
            pl.BlockSpec((2, d), lambda i, j: (0, 0)),
            pl.BlockSpec((d, tn), lambda i, j: (0, j)),
        ],
        out_specs=pl.BlockSpec((tm, tn), lambda i, j: (i, j)),
        out_shape=jax.ShapeDtypeStruct((m, n_cols), BF16),
        scratch_shapes=[pltpu.VMEM((2, tm, d), BF16)],
        compiler_params=_params("parallel", "arbitrary"),
        name="rms_proj2",
    )(h, g2, w)


CONV_LANES = 256
CONV_ROWS = 64


def _conv_kernel(u_ref, uh_ref, halo0_ref, h_ref, dw_ref, dwb_ref, lng_ref, lnb_ref,
                 w2_ref, b2_ref, o_ref, xp_ref, sh_ref, cv_ref, *, tl):
    i = pl.program_id(1)
    d = u_ref.shape[-1]

    @pl.when(i == 0)
    def _():
        xp_ref[0:CONV_HALO, :] = halo0_ref[...].astype(F32)

    @pl.when(i > 0)
    def _():
        xp_ref[0:CONV_HALO, :] = uh_ref[...].astype(F32)

    xp_ref[CONV_HALO:CONV_HALO + tl, :] = u_ref[...].astype(F32)

    n_sh = tl + CONV_HALO - 8
    rows = min(CONV_ROWS, tl)

    def lane_block(cb, carry):
        c0 = pl.multiple_of(cb * CONV_LANES, CONV_LANES)
        for r in range(1, 8):
            sh_ref[r, 0:n_sh, :] = xp_ref[pl.ds(r, n_sh), pl.ds(c0, CONV_LANES)]

        def row_chunk(rc, carry2):
            r0 = pl.multiple_of(rc * rows, rows)
            acc = jnp.zeros((rows, CONV_LANES), F32)
            for w in range(CONV_WIDTH):
                a, r = divmod(CONV_HALO - (CONV_WIDTH - 1) + w, 8)
                if r == 0:
                    xs = xp_ref[pl.ds(r0 + 8 * a, rows), pl.ds(c0, CONV_LANES)]
                else:
                    xs = sh_ref[r, pl.ds(r0 + 8 * a, rows), :]
                acc = acc + xs * dw_ref[pl.ds(w, 1), pl.ds(c0, CONV_LANES)]
            cv_ref[pl.ds(r0, rows), pl.ds(c0, CONV_LANES)] = acc + dwb_ref[:, pl.ds(c0, CONV_LANES)]
            return carry2

        lax.fori_loop(0, tl // rows, row_chunk, 0)
        return carry

    lax.fori_loop(0, d // CONV_LANES, lane_block, 0)

    half = max(tl // 2, 16)
    for r0 in range(0, tl, half):
        c = cv_ref[r0:r0 + half, :]
        mu = jnp.mean(c, axis=-1, keepdims=True)
        cc = c - mu
        var = jnp.mean(cc * cc, axis=-1, keepdims=True)
        y = cc * lax.rsqrt(var + LN_EPS) * lng_ref[...] + lnb_ref[...]
        s = (y * _sigmoid(y)).astype(BF16)
        o_ref[r0:r0 + half, :] = h_ref[r0:r0 + half, :] + _dot(s, w2_ref[...]) + b2_ref[...]


def conv_block(u, halo0, h, dw_w, dw_b, ln_g, ln_b, w2, b2, *, tl):
    b, l, d = u.shape
    hb = tl // CONV_HALO
    return pl.pallas_call(
        functools.partial(_conv_kernel, tl=tl),
        grid=(b, l // tl),
        in_specs=[
            pl.BlockSpec((None, tl, d), lambda bi, i: (bi, i, 0)),
            pl.BlockSpec((None, CONV_HALO, d), lambda bi, i: (bi, jnp.maximum(i * hb - 1, 0), 0)),
            _resident((CONV_HALO, d), lambda bi, i: (0, 0)),
            pl.BlockSpec((None, tl, d), lambda bi, i: (bi, i, 0)),
            _resident((CONV_HALO, d), lambda bi, i: (0, 0)),
            _resident((1, d), lambda bi, i: (0, 0)),
            _resident((1, d), lambda bi, i: (0, 0)),
            _resident((1, d), lambda bi, i: (0, 0)),
            _resident((d, d), lambda bi, i: (0, 0)),
            _resident((1, d), lambda bi, i: (0, 0)),
        ],
        out_specs=pl.BlockSpec((None, tl, d), lambda bi, i: (bi, i, 0)),
        out_shape=jax.ShapeDtypeStruct((b, l, d), F32),
        scratch_shapes=[
            pltpu.VMEM((tl + CONV_HALO, d), F32),
            pltpu.VMEM((8, tl + CONV_HALO, CONV_LANES), F32),
            pltpu.VMEM((tl, d), F32),
        ],
        compiler_params=_params("parallel", "arbitrary"),
        name="conv_block",
    )(u, u, halo0, h, dw_w, dw_b, ln_g, ln_b, w2, b2)


def _swiglu_partial(x, wg_ref, wu_ref, wd_ref):
    half = wg_ref.shape[1] // 2
    acts = []
    for sl in (slice(0, half), slice(half, 2 * half)):
        a = _dot(x, wg_ref[:, sl])
        u = _dot(x, wu_ref[:, sl])
        acts.append((a * _sigmoid(a) * u).astype(BF16))
    return _dot(jnp.concatenate(acts, axis=1), wd_ref[...])


def _ffn_kernel(h_ref, g_ref, wg_ref, wu_ref, wd_ref, o_ref, hn_ref):
    @pl.when(pl.program_id(1) == 0)
    def _():
        h = h_ref[...]
        hn_ref[...] = _rms(h, g_ref[...]).astype(BF16)
        o_ref[...] = h

    o_ref[...] += _swiglu_partial(hn_ref[...], wg_ref, wu_ref, wd_ref)


def ffn_block(h, g, w_gate, w_up, w_down, *, tm, tf):
    m, d = h.shape
    ff = w_gate.shape[1]
    return pl.pallas_call(
        _ffn_kernel,
        grid=(m // tm, ff // tf),
        in_specs=[
            pl.BlockSpec((tm, d), lambda i, f: (i, 0)),
            pl.BlockSpec((1, d), lambda i, f: (0, 0)),
            pl.BlockSpec((d, tf), lambda i, f: (0, f)),
            pl.BlockSpec((d, tf), lambda i, f: (0, f)),
            pl.BlockSpec((tf, d), lambda i, f: (f, 0)),
        ],
        out_specs=pl.BlockSpec((tm, d), lambda i, f: (i, 0)),
        out_shape=jax.ShapeDtypeStruct((m, d), F32),
        scratch_shapes=[pltpu.VMEM((tm, d), BF16)],
        compiler_params=_params("parallel", "arbitrary"),
        name="ffn_block",
    )(h, g, w_gate, w_up, w_down)


ATTN_TILE = 256
ATTN_HEADS = 4
META_KEYS = 128
Q_PRESCALE = -(HEAD_DIM ** -0.5) * LOG2_E


def _attn_kernel(q_ref, k_ref, v_ref, km_ref, vm_ref, *rest, seq, n_cast):
    cast_src = rest[:n_cast]
    o_ref = rest[n_cast]
    cast_dst = rest[n_cast + 1:2 * n_cast + 1]
    cast_in = rest[2 * n_cast + 1:3 * n_cast + 1]
    cast_out = rest[3 * n_cast + 1:4 * n_cast + 1]
    sem_in, sem_out = rest[4 * n_cast + 1:]
    t = ATTN_TILE
    nq = seq // t
    step = pl.program_id(0) * pl.num_programs(1) + pl.program_id(1)

    def chunk_rows(a, chunk):
        rows = cast_in[a].shape[1]
        return pl.ds(pl.multiple_of(chunk * rows, 16), rows)

    def cast_loads(chunk, slot):
        return [pltpu.make_async_copy(cast_src[a].at[chunk_rows(a, chunk), :], cast_in[a].at[slot], sem_in.at[slot, a])
                for a in range(n_cast)]

    def cast_stores(chunk, slot):
        return [pltpu.make_async_copy(cast_out[a].at[slot], cast_dst[a].at[chunk_rows(a, chunk), :], sem_out.at[slot, a])
                for a in range(n_cast)]

    row = lax.broadcasted_iota(jnp.int32, (t, t), 0)
    col = lax.broadcasted_iota(jnp.int32, (t, t), 1)
    later = (row > col).astype(BF16)
    later_m = later[:META_KEYS, :META_KEYS]
    causal = col < row
    meta_ok = lax.broadcasted_iota(jnp.int32, (t, META_KEYS), 1) < N_META

    heads = [slice(hh * HEAD_DIM, (hh + 1) * HEAD_DIM) for hh in range(ATTN_HEADS)]

    def scores(qts, kts, later_mat, mask):
        ys = [lax.dot_general(qt, kt, (((1,), (1,)), ((), ())), preferred_element_type=F32)
              for qt, kt in zip(qts, kts)]
        sums, parts, own = [], [], []
        for y in ys:
            e = jnp.exp2(jnp.minimum(y, -y))
            ls = jnp.minimum(y, 0.0) - jnp.log2(1.0 + e)
            if mask is not None:
                ls = jnp.where(mask, ls, 0.0)
            sums.append(jnp.sum(ls, axis=-1, keepdims=True))
            parts.append(ls.astype(BF16))
            own.append(ls - y)
        logws = []
        for part, o in zip(parts, own):
            logw = o + _dot(part, later_mat)
            if mask is not None:
                logw = jnp.where(mask, logw, -jnp.inf)
            logws.append(logw)
        return sums, logws

    def weighted_values(logws, runs, vts):
        return [_dot(jnp.exp2(logw + run).astype(BF16), vt) for logw, run, vt in zip(logws, runs, vts)]

    def q_loop(i, carry):
        chunk = step * nq + i
        slot = i % 2

        @pl.when(i == 0)
        def _():
            for dma in cast_loads(chunk, slot):
                dma.start()

        @pl.when(i + 1 < nq)
        def _():
            for dma in cast_loads(chunk + 1, 1 - slot):
                dma.start()

        q0 = pl.multiple_of(i * t, t)
        qts = [q_ref[pl.ds(q0, t), sl] for sl in heads]

        def kv(k0):
            return [k_ref[pl.ds(k0, t), sl] for sl in heads], [v_ref[pl.ds(k0, t), sl] for sl in heads]

        kts, vts = kv(q0)
        runs, logws = scores(qts, kts, later, causal)
        _, logws_meta = scores(qts, [km_ref[:, sl] for sl in heads], later_m, meta_ok)
        accs = weighted_values(logws, [0.0] * ATTN_HEADS, vts)

        def kv_loop(jj, c):
            runs, accs = c
            kts, vts = kv(pl.multiple_of((i - 1 - jj) * t, t))
            sums, logws = scores(qts, kts, later, None)
            pvs = weighted_values(logws, runs, vts)
            return [run + s for run, s in zip(runs, sums)], [acc + pv for acc, pv in zip(accs, pvs)]

        runs, accs = lax.fori_loop(0, i, kv_loop, (runs, accs))
        pvs = weighted_values(logws_meta, runs, [vm_ref[:, sl] for sl in heads])
        for sl, acc, pv in zip(heads, accs, pvs):
            o_ref[pl.ds(q0, t), sl] = (acc + pv).astype(o_ref.dtype)

        for dma in cast_loads(chunk, slot):
            dma.wait()

        @pl.when(i >= 2)
        def _():
            for dma in cast_stores(chunk - 2, slot):
                dma.wait()

        for a in range(n_cast):
            cast_out[a][slot] = cast_in[a][slot].astype(BF16)
        for dma in cast_stores(chunk, slot):
            dma.start()
        return carry

    lax.fori_loop(0, nq, q_loop, 0)
    for back in (2, 1):
        for dma in cast_stores(step * nq + nq - back, (nq - back) % 2):
            dma.wait()


def attention(qkv, kv_meta, cast_srcs, *, batch, seq):
    w = ATTN_HEADS * HEAD_DIM
    nb = D_ATTN // w
    n_cast = len(cast_srcs)
    n_chunks = batch * nb * (seq // ATTN_TILE)
    chunk_shapes = [(a.shape[0] // n_chunks, a.shape[1]) for a in cast_srcs]
    any_spec = pl.BlockSpec(memory_space=pl.ANY)
    outs = pl.pallas_call(
        functools.partial(_attn_kernel, seq=seq, n_cast=n_cast),
        grid=(batch, nb),
        in_specs=[
            pl.BlockSpec((seq, w), lambda b, hp: (b, 2 * nb + hp)),
            pl.BlockSpec((seq, w), lambda b, hp: (b, hp)),
            pl.BlockSpec((seq, w), lambda b, hp: (b, nb + hp)),
            pl.BlockSpec((META_KEYS, w), lambda b, hp: (0, hp)),
            pl.BlockSpec((META_KEYS, w), lambda b, hp: (0, nb + hp)),
        ] + [any_spec] * n_cast,
        out_specs=[pl.BlockSpec((seq, w), lambda b, hp: (b, hp))] + [any_spec] * n_cast,
        out_shape=[jax.ShapeDtypeStruct((batch * seq, D_ATTN), BF16)]
        + [jax.ShapeDtypeStruct(a.shape, BF16) for a in cast_srcs],
        scratch_shapes=[pltpu.VMEM((2,) + cs, F32) for cs in chunk_shapes]
        + [pltpu.VMEM((2,) + cs, BF16) for cs in chunk_shapes]
        + [pltpu.SemaphoreType.DMA((2, n_cast)), pltpu.SemaphoreType.DMA((2, n_cast))],
        compiler_params=_params("parallel", "parallel"),
        name="attention",
    )(qkv, qkv, qkv, kv_meta, kv_meta, *cast_srcs)
    return outs[0], outs[1:]


def _oproj_router_kernel(o_ref, wo_ref, h_ref, g_ref, wr_ref, h3_ref, hn_ref, idx_ref, gate_ref):
    wr_hi, wr_lo = _split_bf16(wr_ref[...])
    half = o_ref.shape[0] // 2
    for rows in (slice(0, half), slice(half, 2 * half)):
        h3 = h_ref[rows, :] + _dot(o_ref[rows, :], wo_ref[...])
        h3_ref[rows, :] = h3
        hn = _rms(h3, g_ref[...])
        hn_ref[rows, :] = hn
        hn_hi, hn_lo = _split_bf16(hn)
        logits = _dot(hn_hi, wr_hi) + (_dot(hn_hi, wr_lo) + _dot(hn_lo, wr_hi))
        e = lax.broadcasted_iota(jnp.int32, logits.shape, 1)
        m1 = jnp.max(logits, axis=-1, keepdims=True)
        i1 = jnp.min(jnp.where(logits == m1, e, N_EXPERTS), axis=-1, keepdims=True)
        rest = jnp.where(e == i1, -jnp.inf, logits)
        m2 = jnp.max(rest, axis=-1, keepdims=True)
        i2 = jnp.min(jnp.where(rest == m2, e, N_EXPERTS), axis=-1, keepdims=True)
        p = jnp.exp(m2 - m1)
        g1 = 1.0 / (1.0 + p)
        k = lax.broadcasted_iota(jnp.int32, (half, TOP_K), 1)
        idx_ref[rows, :] = jnp.where(k == 0, i1, i2)
        gate_ref[rows, :] = jnp.where(k == 0, g1, p * g1)


def oproj_router(o, wo, h, g, w_router, *, tm):
    m, d = h.shape
    return pl.pallas_call(
        _oproj_router_kernel,
        grid=(m // tm,),
        in_specs=[
            pl.BlockSpec((tm, d), lambda i: (i, 0)),
            _resident((d, d), lambda i: (0, 0)),
            pl.BlockSpec((tm, d), lambda i: (i, 0)),
            _resident((1, d), lambda i: (0, 0)),
            _resident((d, N_EXPERTS), lambda i: (0, 0)),
        ],
        out_specs=[
            pl.BlockSpec((tm, d), lambda i: (i, 0)),
            pl.BlockSpec((tm, d), lambda i: (i, 0)),
            pl.BlockSpec((tm, TOP_K), lambda i: (i, 0)),
            pl.BlockSpec((tm, TOP_K), lambda i: (i, 0)),
        ],
        out_shape=[
            jax.ShapeDtypeStruct((m, d), F32),
            jax.ShapeDtypeStruct((m, d), F32),
            jax.ShapeDtypeStruct((m, TOP_K), jnp.int32),
            jax.ShapeDtypeStruct((m, TOP_K), F32),
        ],
        compiler_params=_params("parallel"),
        name="oproj_router",
    )(o, wo, h, g, w_router)


SUBLANES = 8
MOE_TILE = 1024
MOE_TILE_PARTS = 4


def _dispatch_kernel(dest_ref, gend_ref, nu_ref, hn_ref, x_hbm, zero_ref, sem, zsem, *, tm, tile, n_tiles):
    base = pl.program_id(0) * (tm * TOP_K)

    @pl.when(pl.program_id(0) == 0)
    def _():
        zero_ref[...] = jnp.zeros_like(zero_ref)

        def zero_tile(start):
            return pltpu.make_async_copy(zero_ref, x_hbm.at[pl.ds(pl.multiple_of(start, tile), tile), :], zsem)

        def nonempty(e):
            return gend_ref[e] > (gend_ref[e - 1] if e else 0)

        for e in range(N_EXPERTS):
            @pl.when(nonempty(e))
            def _():
                zero_tile(gend_ref[e] - tile).start()

        for e in range(N_EXPERTS):
            @pl.when(nonempty(e))
            def _():
                zero_tile(0).wait()

        def tail_start(j, c):
            zero_tile(j * tile).start()
            return c

        def tail_wait(j, c):
            zero_tile(0).wait()
            return c

        lax.fori_loop(nu_ref[0], n_tiles, tail_start, 0)
        lax.fori_loop(nu_ref[0], n_tiles, tail_wait, 0)

    def row(grp, sub, dst):
        return pltpu.make_async_copy(hn_ref.at[grp, pl.ds(sub, 1), :], x_hbm.at[pl.ds(dst, 1), :], sem)

    def issue(grp, c):
        for sub in range(SUBLANES):
            for k in range(TOP_K):
                dst = dest_ref[base + TOP_K * (grp * SUBLANES + sub) + k]
                row(grp, sub, dst).start(priority=k)
        return c

    lax.fori_loop(0, tm // SUBLANES, issue, 0)

    def wait(grp, c):
        for sub in range(SUBLANES):
            for k in range(TOP_K):
                row(grp, sub, 0).wait()
        return c

    lax.fori_loop(0, tm // SUBLANES, wait, 0)


def moe_dispatch(hn, dest, gend, n_used, padded_rows, *, tm, tile):
    t, d = hn.shape
    grid_spec = pltpu.PrefetchScalarGridSpec(
        num_scalar_prefetch=3,
        grid=(t // tm,),
        in_specs=[pl.BlockSpec((tm // SUBLANES, SUBLANES, d), lambda i, *_: (i, 0, 0))],
        out_specs=pl.BlockSpec(memory_space=pl.ANY),
        scratch_shapes=[pltpu.VMEM((tile, d), F32), pltpu.SemaphoreType.DMA(()), pltpu.SemaphoreType.DMA(())],
    )
    return pl.pallas_call(
        functools.partial(_dispatch_kernel, tm=tm, tile=tile, n_tiles=padded_rows // tile),
        grid_spec=grid_spec,
        out_shape=jax.ShapeDtypeStruct((padded_rows, d), F32),
        compiler_params=_params("arbitrary"),
        name="moe_dispatch",
    )(dest.reshape(-1), gend, n_used, hn.reshape(t // SUBLANES, SUBLANES, d))


def _moe_kernel(te_ref, nu_ref, nv_ref, x_ref, wg_ref, wu_ref, wd_ref, o_ref, xb_ref):
    i = pl.program_id(0)
    f = pl.program_id(1)
    part = x_ref.shape[0] // MOE_TILE_PARTS
    n_valid = jnp.where(i < nu_ref[0], nv_ref[i], 0)

    @pl.when(f == 0)
    def _():
        xb_ref[...] = x_ref[...].astype(BF16)
        o_ref[...] = jnp.zeros_like(o_ref)

    for parts in range(1, MOE_TILE_PARTS + 1):
        rows = parts * part

        @pl.when((n_valid > rows - part) & (n_valid <= rows))
        def _():
            o_ref[0:rows, :] += _swiglu_partial(xb_ref[0:rows, :], wg_ref, wu_ref, wd_ref)


def moe_experts(x, tile_e, n_used, n_valid, w_gate, w_up, w_down, *, tm, tf):
    p, d = x.shape
    nf = w_gate.shape[2] // tf

    def wmap_in(i, f, te, nu, nv):
        return (te[i], 0, jnp.where(i < nu[0], f, nf - 1))

    def wmap_down(i, f, te, nu, nv):
        return (te[i], jnp.where(i < nu[0], f, nf - 1), 0)

    grid_spec = pltpu.PrefetchScalarGridSpec(
        num_scalar_prefetch=3,
        grid=(p // tm, nf),
        in_specs=[
            pl.BlockSpec((tm, d), lambda i, f, *_: (i, 0)),
            pl.BlockSpec((None, d, tf), wmap_in),
            pl.BlockSpec((None, d, tf), wmap_in),
            pl.BlockSpec((None, tf, d), wmap_down),
        ],
        out_specs=pl.BlockSpec((tm, d), lambda i, f, *_: (i, 0)),
        scratch_shapes=[pltpu.VMEM((tm, d), BF16)],
    )
    return pl.pallas_call(
        _moe_kernel,
        grid_spec=grid_spec,
        out_shape=jax.ShapeDtypeStruct((p, d), F32),
        compiler_params=_params("arbitrary", "arbitrary"),
        name="moe_experts",
    )(tile_e, n_used, n_valid, x, w_gate, w_up, w_down)


def _combine_kernel(dest_ref, h_ref, gate_ref, g_ref, y_hbm, o_ref, ybuf, sem, *, tm, n_steps):
    i = pl.program_id(0)

    def row(slot, grp, sub, k, src):
        return pltpu.make_async_copy(y_hbm.at[pl.ds(src, 1), :], ybuf.at[slot, k, grp, pl.ds(sub, 1), :], sem.at[slot])

    def fetch(step, slot):
        base = step * (tm * TOP_K)

        def issue(grp, c):
            for sub in range(SUBLANES):
                for k in range(TOP_K):
                    src = dest_ref[base + TOP_K * (grp * SUBLANES + sub) + k]
                    row(slot, grp, sub, k, src).start(priority=k)
            return c

        lax.fori_loop(0, tm // SUBLANES, issue, 0)

    @pl.when(i == 0)
    def _():
        fetch(0, 0)

    @pl.when(i + 1 < n_steps)
    def _():
        fetch(i + 1, (i + 1) % 2)

    slot = i % 2

    def wait(grp, c):
        for sub in range(SUBLANES):
            for k in range(TOP_K):
                row(slot, grp, sub, k, 0).wait()
        return c

    lax.fori_loop(0, tm // SUBLANES, wait, 0)
    gate = gate_ref[...]
    d = h_ref.shape[1]
    y = (h_ref[...] + gate[:, 0:1] * ybuf[slot, 0].reshape(tm, d) + gate[:, 1:2] * ybuf[slot, 1].reshape(tm, d))
    o_ref[...] = _rms(y, g_ref[...])


def moe_combine_norm(h, gates, y, dest, g, *, tm):
    t, d = h.shape
    n_steps = t // tm
    grid_spec = pltpu.PrefetchScalarGridSpec(
        num_scalar_prefetch=1,
        grid=(n_steps,),
        in_specs=[
            pl.BlockSpec((tm, d), lambda i, dest_ref: (i, 0)),
            pl.BlockSpec((tm, TOP_K), lambda i, dest_ref: (i, 0)),
            pl.BlockSpec((1, d), lambda i, dest_ref: (0, 0)),
            pl.BlockSpec(memory_space=pl.ANY),
        ],
        out_specs=pl.BlockSpec((tm, d), lambda i, dest_ref: (i, 0)),
        scratch_shapes=[pltpu.VMEM((2, TOP_K, tm // SUBLANES, SUBLANES, d), F32), pltpu.SemaphoreType.DMA((2,))],
    )
    return pl.pallas_call(
        functools.partial(_combine_kernel, tm=tm, n_steps=n_steps),
        grid_spec=grid_spec,
        out_shape=jax.ShapeDtypeStruct((t, d), F32),
        compiler_params=_params("arbitrary"),
        name="moe_combine_norm",
    )(dest.reshape(-1), h, gates, g, y)


def routing_tables(idx, *, tm):
    t = idx.shape[0]
    rows = TOP_K * t
    n_tiles = rows // tm + N_EXPERTS
    e_flat = idx.reshape(rows)
    onehot = (e_flat[:, None] == jnp.arange(N_EXPERTS, dtype=jnp.int32)[None, :]).astype(jnp.int32)
    csum = jnp.cumsum(onehot, axis=0)
    rank = jnp.sum((csum - onehot) * onehot, axis=1)
    counts = csum[-1]
    padded = ((counts + tm - 1) // tm) * tm
    gend = jnp.cumsum(padded)
    gstart = gend - padded
    dest = (jnp.sum(onehot * gstart[None, :], axis=1) + rank).astype(jnp.int32).reshape(t, TOP_K)
    n_used = (gend[-1] // tm).astype(jnp.int32).reshape(1)
    tile_start = jnp.minimum(jnp.arange(n_tiles, dtype=jnp.int32), n_used - 1) * tm
    tile_e = jnp.sum((tile_start[:, None] >= gend[None, :]).astype(jnp.int32), axis=1).astype(jnp.int32)
    n_valid = jnp.clip(jnp.sum((tile_e[:, None] == jnp.arange(N_EXPERTS)[None, :]) * (gstart + counts)[None, :], axis=1)
                       - tile_start, 0, tm).astype(jnp.int32)
    return dest, tile_e, n_used, n_valid, gend.astype(jnp.int32), n_tiles * tm


def kernel(x, meta_tokens, mix_norm_g, ffn_norm_g, conv_pw1_w, conv_pw1_b, conv_dw_w, conv_dw_b,
           conv_ln_g, conv_ln_b, conv_pw2_w, conv_pw2_b, kv_norm_g, w_kv, w_q, w_o, ffn_w_gate,
           ffn_w_up, ffn_w_down, moe_router, moe_w_gate, moe_w_up, moe_w_down, final_norm_g):
    batch, seq, d = x.shape
    t = batch * seq
    row = lambda v: v.reshape(1, -1).astype(F32)

    pw1 = conv_pw1_w[0].astype(BF16)
    pw1_b = row(conv_pw1_b[0])
    dw_w = jnp.pad(conv_dw_w[0], ((0, CONV_HALO - CONV_WIDTH), (0, 0)))
    pw2 = conv_pw2_w[0].astype(BF16)
    w_kvq = jnp.concatenate([w_kv.astype(BF16), w_q[0].astype(BF16)], axis=1)
    g_kvq = jnp.stack([kv_norm_g, mix_norm_g[1]]).astype(F32)
    wo = w_o[0].astype(BF16)
    fg, fu, fd = ffn_w_gate[0].astype(BF16), ffn_w_up[0].astype(BF16), ffn_w_down[0].astype(BF16)
    moe_w = [moe_w_gate[0], moe_w_up[0], moe_w_down[0]]

    def layer0(h, halo0, *, b, l, tm, tl):
        u = rms_glu(h, row(mix_norm_g[0]), pw1, pw1_b, tm=min(2 * tm, b * l), tn=1024)
        h1 = conv_block(u.reshape(b, l, d), halo0, h.reshape(b, l, d), dw_w, row(conv_dw_b[0]),
                        row(conv_ln_g[0]), row(conv_ln_b[0]), pw2, row(conv_pw2_b[0]), tl=tl)
        h2 = ffn_block(h1.reshape(b * l, d), row(ffn_norm_g[0]), fg, fu, fd, tm=min(2 * tm, b * l), tf=512)
        return u, h2

    hm = jnp.pad(meta_tokens.astype(F32), ((0, META_PAD - N_META), (0, 0)))
    um, hm2 = layer0(hm, jnp.zeros((CONV_HALO, d), BF16), b=1, l=META_PAD, tm=META_PAD, tl=META_PAD)
    kv_meta = rms_proj2(hm2, g_kvq, w_kvq, n_cols=2 * D_ATTN, n_first_cols=2 * D_ATTN, second_scale=1.0,
                        tm=META_PAD, tn=1024)
    kv_meta = jnp.pad(kv_meta[:N_META], ((0, META_KEYS - N_META), (0, 0)))

    halo0 = jnp.concatenate([jnp.zeros((CONV_HALO - N_META, d), BF16), um[:N_META]], axis=0)
    _, h2 = layer0(x.reshape(t, d), halo0, b=batch, l=seq, tm=512, tl=512)

    qkv = rms_proj2(h2, g_kvq, w_kvq, n_cols=3 * D_ATTN, n_first_cols=2 * D_ATTN, second_scale=Q_PRESCALE,
                    tm=1024, tn=1024)
    o, moe_bf16 = attention(qkv, kv_meta, [w.reshape(-1, w.shape[-1]) for w in moe_w], batch=batch, seq=seq)
    mg, mu, md = [wb.reshape(w.shape) for wb, w in zip(moe_bf16, moe_w)]
    h3, hn3, idx, gates = oproj_router(o, wo, h2, row(ffn_norm_g[1]), moe_router[0].astype(F32), tm=512)

    dest, tile_e, n_used, n_valid, gend, padded_rows = routing_tables(idx, tm=MOE_TILE)
    xs = moe_dispatch(hn3, dest, gend, n_used, padded_rows, tm=512, tile=MOE_TILE)
    ys = moe_experts(xs, tile_e, n_used, n_valid, mg, mu, md, tm=MOE_TILE, tf=512)
    out = moe_combine_norm(h3, gates, ys, dest, row(final_norm_g), tm=256)
    return out.reshape(batch, seq, d)
```
